```python
import jax, jax.numpy as jnp
from jax import lax
import numpy as np

D_MODEL = 4096
BATCH = 2
SEQ = 8192
DEPTH = 2

GRID_W = 64
CTX_LEN = 256
CONV_DIM = 1024
CONV_WIDTH = 3
MLA_HEADS = 16
QK_NOPE_DIM = 128
QK_ROPE_DIM = 64
V_HEAD_DIM = 128
Q_LORA_RANK = 1024
KV_LORA_RANK = 512
ROPE_THETA = 10000.0
Q_BLOCK = 128
POOL_WINDOWS = (2, 4, 8, 16)
POOL_GROUPS = 4
POOL_GROUP_DIM = 256
POOL_DIM = POOL_GROUPS * POOL_GROUP_DIM
N_BRANCHES = 3
N_MOD = 6
D_FF = (8 * D_MODEL + 3 * 256 - 1) // (3 * 256) * 256
NORM_EPS = 1e-6

COL_CONV = 0
COL_Q = COL_CONV + 3 * CONV_DIM
COL_KV = COL_Q + Q_LORA_RANK
COL_KPE = COL_KV + KV_LORA_RANK
COL_POOL = COL_KPE + QK_ROPE_DIM
COL_GATE = COL_POOL + POOL_DIM
IN_DIM = COL_GATE + N_BRANCHES * D_MODEL

kernel_name = 'hybrid_gated_conv_mla_pool_dit'


def rmsnorm(x, w):
    xf = x.astype(jnp.float32)
    y = xf * lax.rsqrt(jnp.mean(xf * xf, axis=-1, keepdims=True) + NORM_EPS)
    return (y * w.astype(jnp.float32)).astype(x.dtype)


def modulate(x, norm_w, shift, scale):
    return rmsnorm(x, norm_w) * (1 + scale) + shift


def axial_rope_tables(n_tokens, dtype):
    rows = n_tokens // GRID_W
    row = jnp.repeat(jnp.arange(rows, dtype=jnp.int32), GRID_W)
    col = jnp.broadcast_to(jnp.arange(GRID_W, dtype=jnp.int32)[None, :], (rows, GRID_W)).reshape(-1)
    axis_dim = QK_ROPE_DIM // 2
    inv_freq = 1.0 / (ROPE_THETA ** (jnp.arange(0, axis_dim, 2, dtype=jnp.float32) / axis_dim))
    ang = jnp.stack([row.astype(jnp.float32)[:, None] * inv_freq,
                     col.astype(jnp.float32)[:, None] * inv_freq], axis=1)
    return jnp.cos(ang).astype(dtype), jnp.sin(ang).astype(dtype)


def apply_axial_rope(x, cos, sin):
    half = QK_ROPE_DIM // 4
    xs = x.reshape(*x.shape[:-1], 2, 2, half)
    x1, x2 = xs[..., 0, :], xs[..., 1, :]
    bshape = (cos.shape[0],) + (1,) * (x.ndim - 3) + (2, half)
    cs, sn = cos.reshape(bshape), sin.reshape(bshape)
    out = jnp.stack([x1 * cs - x2 * sn, x1 * sn + x2 * cs], axis=-2)
    return out.reshape(x.shape)


def mla_queries(cq, q_norm_w, w_uq, rope):
    b, s, _ = cq.shape
    q = (rmsnorm(cq, q_norm_w) @ w_uq).reshape(b, s, MLA_HEADS, QK_NOPE_DIM + QK_ROPE_DIM)
    q_nope, q_pe = q[..., :QK_NOPE_DIM], q[..., QK_NOPE_DIM:]
    if rope is not None:
        q_pe = apply_axial_rope(q_pe, *rope)
    return q_nope, q_pe


def mla_keys_values(ckv, k_pe, kv_norm_w, w_ukv, rope):
    b, s, _ = ckv.shape
    kv = (rmsnorm(ckv, kv_norm_w) @ w_ukv).reshape(b, s, MLA_HEADS, QK_NOPE_DIM + V_HEAD_DIM)
    k_nope, v = kv[..., :QK_NOPE_DIM], kv[..., QK_NOPE_DIM:]
    if rope is not None:
        k_pe = apply_axial_rope(k_pe, *rope)
    return k_nope, k_pe, v


def mla_attend(q_nope, q_pe, k_nope, k_pe, v):
    scale = (QK_NOPE_DIM + QK_ROPE_DIM) ** -0.5
    s = (jnp.einsum('bqhd,bkhd->bhqk', q_nope, k_nope)
         + jnp.einsum('bqhr,bkr->bhqk', q_pe, k_pe))
    p = jax.nn.softmax(s.astype(jnp.float32) * scale, axis=-1).astype(v.dtype)
    return jnp.einsum('bhqk,bkhd->bqhd', p, v)


def blocked_attention(q_nope, q_pe, k_nope, k_pe, v):
    b, s, h, _ = q_nope.shape
    nb = s // Q_BLOCK

    def to_blocks(t):
        return t.reshape(b, nb, Q_BLOCK, *t.shape[2:]).swapaxes(0, 1)

    out = lax.map(lambda qs: mla_attend(qs[0], qs[1], k_nope, k_pe, v),
                  (to_blocks(q_nope), to_blocks(q_pe)))
    return out.swapaxes(0, 1).reshape(b, s, h * V_HEAD_DIM)


def short_conv_branch(f_conv, conv_w):
    bg = f_conv[..., :CONV_DIM]
    cg = f_conv[..., CONV_DIM:2 * CONV_DIM]
    xin = f_conv[..., 2 * CONV_DIM:]
    u = cg * xin
    pad = CONV_WIDTH // 2
    s = u.shape[1]
    up = jnp.pad(u, ((0, 0), (pad, pad), (0, 0)))
    y = sum(up[:, k:k + s] * conv_w[k] for k in range(CONV_WIDTH))
    return bg * y


def pool_branch(u, pool_w, pool_scale):
    b, s, _ = u.shape
    g = u.reshape(b, s, POOL_GROUPS, POOL_GROUP_DIM).astype(jnp.float32)
    csum = jnp.pad(jnp.cumsum(g, axis=1), ((0, 0), (1, 0), (0, 0), (0, 0)))
    t = jnp.arange(s, dtype=jnp.int32)
    pooled = []
    for gi, w in enumerate(POOL_WINDOWS):
        lo = jnp.maximum(t - w // 2, 0)
        hi = jnp.minimum(t + (w - w // 2), s)
        cs = csum[:, :, gi]
        win_sum = cs[:, hi] - cs[:, lo]
        cnt = (hi - lo).astype(jnp.float32)[None, :, None]
        pooled.append(win_sum / cnt - g[:, :, gi])
    p = jnp.stack(pooled, axis=2).astype(u.dtype)
    y = jnp.einsum('bsgi,gio->bsgo', p, pool_w).reshape(b, s, POOL_DIM)
    return y * pool_scale


def mix_stream(f, attn_y, conv_w, pool_w, pool_scale, w_conv_out, w_mla_out, w_pool_out, w_o):
    b, s, _ = f.shape
    conv_y = short_conv_branch(f[..., COL_CONV:COL_Q], conv_w)
    pool_y = pool_branch(f[..., COL_POOL:COL_GATE], pool_w, pool_scale)
    gates = jax.nn.sigmoid(f[..., COL_GATE:]).reshape(b, s, N_BRANCHES, D_MODEL)
    merged = (gates[:, :, 0] * (conv_y @ w_conv_out)
              + gates[:, :, 1] * (attn_y @ w_mla_out)
              + gates[:, :, 2] * (pool_y @ w_pool_out))
    return merged @ w_o


def swiglu(h, w_gate, w_up, w_down):
    return (jax.nn.silu(h @ w_gate) * (h @ w_up)) @ w_down


def setup_inputs(seed: int = 0) -> dict:
    key = jax.random.key(seed)
    ks = jax.random.split(key, 26)
    L, D = DEPTH, D_MODEL

    def nrm(k, shape, scale):
        return jax.random.normal(k, shape, jnp.float32) * scale

    return {
        'x': nrm(ks[0], (BATCH, SEQ, D), 1.0),
        'c': nrm(ks[1], (BATCH, D), 1.0),
        'ctx': nrm(ks[2], (BATCH, CTX_LEN, D), 1.0),
        'c_ctx': nrm(ks[3], (D,), 1.0),
        'w_mod': nrm(ks[4], (L, D, N_MOD * D), 0.5 * D ** -0.5),
        'b_mod': nrm(ks[5], (L, N_MOD * D), 0.01),
        'norm_mix_w': 1.0 + nrm(ks[6], (L, D), 0.01),
        'norm_ffn_w': 1.0 + nrm(ks[7], (L, D), 0.01),
        'w_in': nrm(ks[8], (L, D, IN_DIM), D ** -0.5),
        'conv_w': nrm(ks[9], (L, CONV_WIDTH, CONV_DIM), CONV_WIDTH ** -0.5),
        'w_conv_out': nrm(ks[10], (L, CONV_DIM, D), CONV_DIM ** -0.5),
        'q_norm_w': 1.0 + nrm(ks[11], (L, Q_LORA_RANK), 0.01),
        'kv_norm_w': 1.0 + nrm(ks[12], (L, KV_LORA_RANK), 0.01),
        'w_uq': nrm(ks[13], (L, Q_LORA_RANK, MLA_HEADS * (QK_NOPE_DIM + QK_ROPE_DIM)), Q_LORA_RANK ** -0.5),
        'w_ukv': nrm(ks[14], (L, KV_LORA_RANK, MLA_HEADS * (QK_NOPE_DIM + V_HEAD_DIM)), KV_LORA_RANK ** -0.5),
        'w_mla_out': nrm(ks[15], (L, MLA_HEADS * V_HEAD_DIM, D), (MLA_HEADS * V_HEAD_DIM) ** -0.5),
        'pool_w': nrm(ks[16], (L, POOL_GROUPS, POOL_GROUP_DIM, POOL_GROUP_DIM), POOL_GROUP_DIM ** -0.5),
        'pool_scale': 1.0 + nrm(ks[17], (L, POOL_DIM), 0.1),
        'w_pool_out': nrm(ks[18], (L, POOL_DIM, D), POOL_DIM ** -0.5),
        'w_o': nrm(ks[19], (L, D, D), D ** -0.5),
        'w_ffn_gate': nrm(ks[20], (L, D, D_FF), D ** -0.5),
        'w_ffn_up': nrm(ks[21], (L, D, D_FF), D ** -0.5),
        'w_ffn_down': nrm(ks[22], (L, D_FF, D), D_FF ** -0.5),
        'final_norm_w': 1.0 + nrm(ks[23], (D,), 0.01),
    }


def reference(x, c, ctx, c_ctx, w_mod, b_mod, norm_mix_w, norm_ffn_w, w_in, conv_w, w_conv_out,
              q_norm_w, kv_norm_w, w_uq, w_ukv, w_mla_out, pool_w, pool_scale, w_pool_out, w_o,
              w_ffn_gate, w_ffn_up, w_ffn_down, final_norm_w):
    b, n_lat, _ = x.shape
    n_ctx = ctx.shape[1]
    rope = axial_rope_tables(n_lat, x.dtype)
    silu_c = jax.nn.silu(c)
    silu_cc = jax.nn.silu(c_ctx)
    h_ctx = ctx
    for l in range(DEPTH):
        last = l == DEPTH - 1
        mod = (silu_c @ w_mod[l] + b_mod[l]).reshape(b, 1, N_MOD, D_MODEL)
        sh1, sc1, g1, sh2, sc2, g2 = (mod[:, :, i] for i in range(N_MOD))
        mod_c = (silu_cc @ w_mod[l] + b_mod[l]).reshape(N_MOD, D_MODEL)
        csh1, csc1, cg1, csh2, csc2, cg2 = (mod_c[i] for i in range(N_MOD))
        w_in_l = w_in[l]
        mixer_w = (conv_w[l], pool_w[l], pool_scale[l], w_conv_out[l], w_mla_out[l], w_pool_out[l], w_o[l])

        hc = modulate(h_ctx, norm_mix_w[l], csh1, csc1)
        if last:
            fc_kv = hc @ w_in_l[:, COL_KV:COL_POOL]
        else:
            fc = hc @ w_in_l
            fc_kv = fc[..., COL_KV:COL_POOL]
        kn_c, kp_c, v_c = mla_keys_values(fc_kv[..., :KV_LORA_RANK], fc_kv[..., KV_LORA_RANK:],
                                          kv_norm_w[l], w_ukv[l], None)

        h = modulate(x, norm_mix_w[l], sh1, sc1)
        f = h @ w_in_l
        qn, qp = mla_queries(f[..., COL_Q:COL_KV], q_norm_w[l], w_uq[l], rope)
        kn, kp, v = mla_keys_values(f[..., COL_KV:COL_KPE], f[..., COL_KPE:COL_POOL],
                                    kv_norm_w[l], w_ukv[l], rope)
        attn = blocked_attention(qn, qp,
                                 jnp.concatenate([kn_c, kn], axis=1),
                                 jnp.concatenate([kp_c, kp], axis=1),
                                 jnp.concatenate([v_c, v], axis=1))
        x = x + g1 * mix_stream(f, attn, *mixer_w)
        x = x + g2 * swiglu(modulate(x, norm_ffn_w[l], sh2, sc2), w_ffn_gate[l], w_ffn_up[l], w_ffn_down[l])

        if not last:
            qn_c, qp_c = mla_queries(fc[..., COL_Q:COL_KV], q_norm_w[l], w_uq[l], None)
            attn_c = mla_attend(qn_c, qp_c, kn_c, kp_c, v_c).reshape(b, n_ctx, MLA_HEADS * V_HEAD_DIM)
            h_ctx = h_ctx + cg1 * mix_stream(fc, attn_c, *mixer_w)
            h_ctx = h_ctx + cg2 * swiglu(modulate(h_ctx, norm_ffn_w[l], csh2, csc2),
                                         w_ffn_gate[l], w_ffn_up[l], w_ffn_down[l])
    return rmsnorm(x, final_norm_w)
```

```python
import functools

import jax
import jax.numpy as jnp
from jax import lax
from jax.experimental import pallas as pl
from jax.experimental.pallas import tpu as pltpu

F32 = jnp.float32
BF16 = jnp.bfloat16

D_MODEL = 4096
DEPTH = 2
GRID_W = 64
CONV_DIM = 1024
MLA_HEADS = 16
QK_NOPE_DIM = 128
QK_ROPE_DIM = 64
V_HEAD_DIM = 128
Q_LORA_RANK = 1024
KV_LORA_RANK = 512
ROPE_THETA = 10000.0
POOL_WINDOWS = (2, 4, 8, 16)
POOL_GROUP_DIM = 256
POOL_DIM = len(POOL_WINDOWS) * POOL_GROUP_DIM
N_BRANCHES = 3
N_MOD = 6
NORM_EPS = 1e-6

COL_Q = 3 * CONV_DIM
COL_KV = COL_Q + Q_LORA_RANK
COL_KPE = COL_KV + KV_LORA_RANK
COL_POOL = COL_KPE + QK_ROPE_DIM
COL_GATE = COL_POOL + POOL_DIM

LANES = 128
QK_PAD = 2 * LANES
HALO = 16
VMEM_LIMIT = 56 * 1024 * 1024
SM_SCALE = (QK_NOPE_DIM + QK_ROPE_DIM) ** -0.5


def _params(*sem):
    return pltpu.CompilerParams(dimension_semantics=sem, vmem_limit_bytes=VMEM_LIMIT)


def _sigmoid(x):
    return 1.0 / (1.0 + jnp.exp(-x))


def _dot(a, b):
    return jnp.dot(a, b, preferred_element_type=F32)


def _rms(x, w):
    ms = jnp.mean(x * x, axis=-1, keepdims=True)
    return x * lax.rsqrt(ms + NORM_EPS) * w


def _rope(pe, cos, sin_lo, sin_hi):
    return (pe * cos + pltpu.roll(pe, LANES - 16, 1) * sin_lo
            + pltpu.roll(pe, 16, 1) * sin_hi)


def _mod_kernel(c_ref, w_ref, b_ref, o_ref):
    c = c_ref[...]
    a = (c * _sigmoid(c)).astype(BF16)
    o_ref[0] = _dot(a, w_ref[0].astype(BF16)) + b_ref[0]


def _mod(cvec, w_mod, b_mod):
    depth, d, n = w_mod.shape
    rows = cvec.shape[0]
    tn = 512
    return pl.pallas_call(
        _mod_kernel,
        grid=(depth, n // tn),
        in_specs=[pl.BlockSpec((rows, d), lambda l, j: (0, 0)),
                  pl.BlockSpec((1, d, tn), lambda l, j: (l, 0, j)),
                  pl.BlockSpec((1, 1, tn), lambda l, j: (l, 0, j))],
        out_specs=pl.BlockSpec((1, rows, tn), lambda l, j: (l, 0, j)),
        out_shape=jax.ShapeDtypeStruct((depth, rows, n), F32),
        compiler_params=_params("arbitrary", "arbitrary"),
        name="mod",
    )(cvec, w_mod, b_mod.reshape(depth, 1, n))


def _norm_mod_kernel(x_ref, w_ref, sh_ref, sc_ref, o_ref):
    y = _rms(x_ref[0], w_ref[...])
    o_ref[0] = (y * (1.0 + sc_ref[0]) + sh_ref[0]).astype(o_ref.dtype)


def _norm_mod(x, w, shift, scale, tm):
    b, s, d = x.shape
    vec = pl.BlockSpec((1, 1, d), lambda bi, i: (bi, 0, 0))
    return pl.pallas_call(
        _norm_mod_kernel,
        grid=(b, s // tm),
        in_specs=[pl.BlockSpec((1, tm, d), lambda bi, i: (bi, i, 0)),
                  pl.BlockSpec((1, d), lambda bi, i: (0, 0)), vec, vec],
        out_specs=pl.BlockSpec((1, tm, d), lambda bi, i: (bi, i, 0)),
        out_shape=jax.ShapeDtypeStruct((b, s, d), BF16),
        compiler_params=_params("arbitrary", "arbitrary"),
        name="norm_mod",
    )(x, w.reshape(1, d), shift, scale)


def _norm_kernel(x_ref, w_ref, o_ref):
    o_ref[...] = _rms(x_ref[...], w_ref[...])


def _final_norm(x, w, tm):
    m, d = x.shape
    return pl.pallas_call(
        _norm_kernel,
        grid=(m // tm,),
        in_specs=[pl.BlockSpec((tm, d), lambda i: (i, 0)),
                  pl.BlockSpec((1, d), lambda i: (0, 0))],
        out_specs=pl.BlockSpec((tm, d), lambda i: (i, 0)),
        out_shape=jax.ShapeDtypeStruct((m, d), F32),
        compiler_params=_params("arbitrary"),
        name="final_norm",
    )(x, w.reshape(1, d))


def _mm_kernel(a_ref, w_ref, o_ref):
    o_ref[...] = _dot(a_ref[...], w_ref[...]).astype(o_ref.dtype)


def _matmul(a, w, tm, tn, name):
    m, k = a.shape
    n = w.shape[1]
    return pl.pallas_call(
        _mm_kernel,
        grid=(m // tm, n // tn),
        in_specs=[pl.BlockSpec((tm, k), lambda i, j: (i, 0)),
                  pl.BlockSpec((k, tn), lambda i, j: (0, j))],
        out_specs=pl.BlockSpec((tm, tn), lambda i, j: (i, j)),
        out_shape=jax.ShapeDtypeStruct((m, n), BF16),
        compiler_params=_params("arbitrary", "arbitrary"),
        name=name,
    )(a, w)


def _cq_kernel(a_ref, w_ref, nw_ref, o_ref):
    o_ref[...] = _rms(_dot(a_ref[...], w_ref[...]), nw_ref[...]).astype(o_ref.dtype)


def _cq_proj(h, w, nw, tm):
    m, k = h.shape
    n = w.shape[1]
    return pl.pallas_call(
        _cq_kernel,
        grid=(m // tm,),
        in_specs=[pl.BlockSpec((tm, k), lambda i: (i, 0)),
                  pl.BlockSpec((k, n), lambda i: (0, 0)),
                  pl.BlockSpec((1, n), lambda i: (0, 0))],
        out_specs=pl.BlockSpec((tm, n), lambda i: (i, 0)),
        out_shape=jax.ShapeDtypeStruct((m, n), BF16),
        compiler_params=_params("arbitrary"),
        name="cq_proj",
    )(h, w, nw.reshape(1, n))


def _ckv_kernel(a_ref, w_ref, nw_ref, cos_ref, slo_ref, shi_ref, ckv_ref, kpe_ref):
    r = _dot(a_ref[...], w_ref[...])
    ckv_ref[...] = _rms(r[:, :KV_LORA_RANK], nw_ref[...]).astype(ckv_ref.dtype)
    pe = _rope(r[:, KV_LORA_RANK:], cos_ref[...], slo_ref[...], shi_ref[...])
    kpe_ref[...] = pe.astype(kpe_ref.dtype)


def _ckv_proj(h, w, nw, rope, tm):
    m, k = h.shape
    n = w.shape[1]
    n_pos = rope[0].shape[0] // tm
    tab = pl.BlockSpec((tm, LANES), lambda i: (i % n_pos, 0))
    return pl.pallas_call(
        _ckv_kernel,
        grid=(m // tm,),
        in_specs=[pl.BlockSpec((tm, k), lambda i: (i, 0)),
                  pl.BlockSpec((k, n), lambda i: (0, 0)),
                  pl.BlockSpec((1, KV_LORA_RANK), lambda i: (0, 0)), tab, tab, tab],
        out_specs=[pl.BlockSpec((tm, KV_LORA_RANK), lambda i: (i, 0)),
                   pl.BlockSpec((tm, LANES), lambda i: (i, 0))],
        out_shape=[jax.ShapeDtypeStruct((m, KV_LORA_RANK), BF16),
                   jax.ShapeDtypeStruct((m, LANES), BF16)],
        compiler_params=_params("arbitrary"),
        name="ckv_proj",
    )(h, w, nw.reshape(1, KV_LORA_RANK), *rope)


def _q_up_kernel(a_ref, w_ref, cos_ref, slo_ref, shi_ref, q_ref):
    a = a_ref[0]
    cos, slo, shi = cos_ref[...], slo_ref[...], shi_ref[...]
    for h in range(MLA_HEADS):
        r = _dot(a, w_ref[h])
        q_ref[0, h, :, :LANES] = (r[:, :LANES] * SM_SCALE).astype(q_ref.dtype)
        pe = _rope(r[:, LANES:], cos, slo, shi)
        q_ref[0, h, :, LANES:] = (pe * SM_SCALE).astype(q_ref.dtype)


def _q_up(cqn, w, rope, tm):
    b, s, k = cqn.shape
    tab = pl.BlockSpec((tm, LANES), lambda bi, i: (i, 0))
    return pl.pallas_call(
        _q_up_kernel,
        grid=(b, s // tm),
        in_specs=[pl.BlockSpec((1, tm, k), lambda bi, i: (bi, i, 0)),
                  pl.BlockSpec((MLA_HEADS, k, QK_PAD), lambda bi, i: (0, 0, 0)), tab, tab, tab],
        out_specs=pl.BlockSpec((1, MLA_HEADS, tm, QK_PAD), lambda bi, i: (bi, 0, i, 0)),
        out_shape=jax.ShapeDtypeStruct((b, MLA_HEADS, s, QK_PAD), BF16),
        compiler_params=_params("arbitrary", "arbitrary"),
        name="q_up",
    )(cqn, w, *rope)


def _kv_up_kernel(a_ref, pe_ref, w_ref, k_ref, v_ref):
    a = a_ref[0]
    pe = pe_ref[0]
    width = QK_NOPE_DIM + V_HEAD_DIM
    for h in range(MLA_HEADS):
        r = _dot(a, w_ref[:, h * width:(h + 1) * width])
        k_ref[0, h, :, :LANES] = r[:, :QK_NOPE_DIM].astype(k_ref.dtype)
        k_ref[0, h, :, LANES:] = pe
        v_ref[0, h] = r[:, QK_NOPE_DIM:].astype(v_ref.dtype)


def _kv_up(ckvn, kpe, w, tm):
    b, s, k = ckvn.shape
    return pl.pallas_call(
        _kv_up_kernel,
        grid=(b, s // tm),
        in_specs=[pl.BlockSpec((1, tm, k), lambda bi, i: (bi, i, 0)),
                  pl.BlockSpec((1, tm, LANES), lambda bi, i: (bi, i, 0)),
                  pl.BlockSpec(w.shape, lambda bi, i: (0, 0))],
        out_specs=[pl.BlockSpec((1, MLA_HEADS, tm, QK_PAD), lambda bi, i: (bi, 0, i, 0)),
                   pl.BlockSpec((1, MLA_HEADS, tm, V_HEAD_DIM), lambda bi, i: (bi, 0, i, 0))],
        out_shape=[jax.ShapeDtypeStruct((b, MLA_HEADS, s, QK_PAD), BF16),
                   jax.ShapeDtypeStruct((b, MLA_HEADS, s, V_HEAD_DIM), BF16)],
        compiler_params=_params("arbitrary", "arbitrary"),
        name="kv_up",
    )(ckvn, kpe, w)


def _flash_init(m_ref, l_ref, acc_ref):
    m_ref[...] = jnp.full(m_ref.shape, -jnp.inf, F32)
    l_ref[...] = jnp.zeros(l_ref.shape, F32)
    acc_ref[...] = jnp.zeros(acc_ref.shape, F32)


def _flash_update(q, k, v, m_ref, l_ref, acc_ref):
    s = lax.dot_general(q, k, (((1,), (1,)), ((), ())), preferred_element_type=F32)
    m_prev = m_ref[...]
    m_new = jnp.maximum(m_prev, jnp.max(s, axis=-1, keepdims=True))
    alpha = jnp.exp(m_prev - m_new)
    p = jnp.exp(s - m_new)
    l_ref[...] = alpha * l_ref[...] + jnp.sum(p, axis=-1, keepdims=True)
    acc_ref[...] = alpha * acc_ref[...] + _dot(p.astype(BF16), v)
    m_ref[...] = m_new


def _flash_kernel(q_ref, kc_ref, vc_ref, kl_ref, vl_ref, o_ref, m_ref, l_ref, acc_ref):
    kv = pl.program_id(3)

    @pl.when(kv == 0)
    def _():
        _flash_init(m_ref, l_ref, acc_ref)
        _flash_update(q_ref[0, 0], kc_ref[0, 0], vc_ref[0, 0], m_ref, l_ref, acc_ref)

    @pl.when(kv > 0)
    def _():
        _flash_update(q_ref[0, 0], kl_ref[0, 0], vl_ref[0, 0], m_ref, l_ref, acc_ref)

    @pl.when(kv == pl.num_programs(3) - 1)
    def _():
        o_ref[0] = (acc_ref[...] / l_ref[...]).astype(o_ref.dtype)


def _flash_ctx_kernel(q_ref, kc_ref, vc_ref, o_ref, m_ref, l_ref, acc_ref):
    _flash_init(m_ref, l_ref, acc_ref)
    _flash_update(q_ref[0, 0], kc_ref[0, 0], vc_ref[0, 0], m_ref, l_ref, acc_ref)
    o_ref[0] = (acc_ref[...] / l_ref[...]).astype(o_ref.dtype)


def _flash_scratch(tq):
    return [pltpu.VMEM((tq, 1), F32), pltpu.VMEM((tq, 1), F32), pltpu.VMEM((tq, V_HEAD_DIM), F32)]


def _flash(q, kc, vc, kl, vl, tq, tk):
    b, nh, s, _ = q.shape
    n_ctx = kc.shape[2]
    n_lat = kl.shape[2]
    lat = lambda bi, h, i, kv: (bi, h, jnp.maximum(kv - 1, 0), 0)
    ctx = lambda bi, h, i, kv: (bi, h, 0, 0)
    return pl.pallas_call(
        _flash_kernel,
        grid=(b, nh, s // tq, 1 + n_lat // tk),
        in_specs=[pl.BlockSpec((1, 1, tq, QK_PAD), lambda bi, h, i, kv: (bi, h, i, 0)),
                  pl.BlockSpec((1, 1, n_ctx, QK_PAD), ctx),
                  pl.BlockSpec((1, 1, n_ctx, V_HEAD_DIM), ctx),
                  pl.BlockSpec((1, 1, tk, QK_PAD), lat),
                  pl.BlockSpec((1, 1, tk, V_HEAD_DIM), lat)],
        out_specs=pl.BlockSpec((1, tq, V_HEAD_DIM), lambda bi, h, i, kv: (bi, i, h)),
        out_shape=jax.ShapeDtypeStruct((b, s, nh * V_HEAD_DIM), BF16),
        scratch_shapes=_flash_scratch(tq),
        compiler_params=_params("arbitrary", "arbitrary", "arbitrary", "arbitrary"),
        name="flash",
    )(q, kc, vc, kl, vl)


def _flash_ctx(q, kc, vc):
    b, nh, s, _ = q.shape
    blk = lambda w: pl.BlockSpec((1, 1, s, w), lambda bi, h: (bi, h, 0, 0))
    return pl.pallas_call(
        _flash_ctx_kernel,
        grid=(b, nh),
        in_specs=[blk(QK_PAD), blk(QK_PAD), blk(V_HEAD_DIM)],
        out_specs=pl.BlockSpec((1, s, V_HEAD_DIM), lambda bi, h: (bi, 0, h)),
        out_shape=jax.ShapeDtypeStruct((b, s, nh * V_HEAD_DIM), BF16),
        scratch_shapes=_flash_scratch(s),
        compiler_params=_params("arbitrary", "arbitrary"),
        name="flash_ctx",
    )(q, kc, vc)


def _conv_kernel(bg_ref, cg_ref, xin_ref, cgp_ref, xinp_ref, cgn_ref, xinn_ref, w_ref, o_ref):
    i = pl.program_id(1)
    tm = cg_ref.shape[1]
    u = cg_ref[0].astype(F32) * xin_ref[0].astype(F32)
    u_prev = cgp_ref[0, HALO - 1:HALO, :].astype(F32) * xinp_ref[0, HALO - 1:HALO, :].astype(F32)
    u_next = cgn_ref[0, 0:1, :].astype(F32) * xinn_ref[0, 0:1, :].astype(F32)
    u_prev = jnp.where(i > 0, u_prev, 0.0)
    u_next = jnp.where(i < pl.num_programs(1) - 1, u_next, 0.0)
    row = lax.broadcasted_iota(jnp.int32, u.shape, 0)
    below = jnp.where(row == 0, u_prev, pltpu.roll(u, 1, 0))
    above = jnp.where(row == tm - 1, u_next, pltpu.roll(u, tm - 1, 0))
    y = below * w_ref[0:1, :] + u * w_ref[1:2, :] + above * w_ref[2:3, :]
    o_ref[0] = (bg_ref[0].astype(F32) * y).astype(o_ref.dtype)


def _conv_branch(f_conv, conv_w, tm):
    b, s, _ = f_conv.shape
    c = CONV_DIM
    hb = tm // HALO
    n_hb = s // HALO
    main = lambda col: pl.BlockSpec((1, tm, c), lambda bi, i: (bi, i, col))
    prev = lambda col: pl.BlockSpec((1, HALO, c), lambda bi, i: (bi, jnp.maximum(i * hb - 1, 0), col))
    nxt = lambda col: pl.BlockSpec((1, HALO, c), lambda bi, i: (bi, jnp.minimum((i + 1) * hb, n_hb - 1), col))
    return pl.pallas_call(
        _conv_kernel,
        grid=(b, s // tm),
        in_specs=[main(0), main(1), main(2), prev(1), prev(2), nxt(1), nxt(2),
                  pl.BlockSpec(conv_w.shape, lambda bi, i: (0, 0))],
        out_specs=pl.BlockSpec((1, tm, c), lambda bi, i: (bi, i, 0)),
        out_shape=jax.ShapeDtypeStruct((b, s, c), BF16),
        compiler_params=_params("arbitrary", "arbitrary"),
        name="conv",
    )(f_conv, f_conv, f_conv, f_conv, f_conv, f_conv, f_conv, conv_w)


def _pool_kernel(u_ref, up_ref, un_ref, w_ref, sc_ref, o_ref, *, seq_len):
    i = pl.program_id(1)
    tm = u_ref.shape[1]
    rows = tm + 2 * HALO
    prev = jnp.where(i > 0, up_ref[0].astype(F32), 0.0)
    nxt = jnp.where(i < pl.num_programs(1) - 1, un_ref[0].astype(F32), 0.0)
    ext = jnp.concatenate([prev, u_ref[0].astype(F32), nxt], axis=0)
    t = i * tm + lax.broadcasted_iota(jnp.int32, (tm, 1), 0)
    g = POOL_GROUP_DIM
    for gi, win in enumerate(POOL_WINDOWS):
        e = ext[:, gi * g:(gi + 1) * g]
        acc = e + pltpu.roll(e, 1, 0)
        half = 1
        while 2 * half < win:
            acc = pltpu.roll(acc, half, 0) + pltpu.roll(acc, rows - half, 0)
            half *= 2
        lo = jnp.maximum(t - win // 2, 0)
        hi = jnp.minimum(t + (win - win // 2), seq_len)
        cnt = (hi - lo).astype(F32)
        p = acc[HALO:HALO + tm] / cnt - e[HALO:HALO + tm]
        y = _dot(p.astype(BF16), w_ref[gi]) * sc_ref[:, gi * g:(gi + 1) * g]
        o_ref[0, :, gi * g:(gi + 1) * g] = y.astype(o_ref.dtype)


def _pool_branch(f_pool, pool_w, pool_scale, tm):
    b, s, c = f_pool.shape
    hb = tm // HALO
    n_hb = s // HALO
    return pl.pallas_call(
        functools.partial(_pool_kernel, seq_len=s),
        grid=(b, s // tm),
        in_specs=[pl.BlockSpec((1, tm, c), lambda bi, i: (bi, i, 0)),
                  pl.BlockSpec((1, HALO, c), lambda bi, i: (bi, jnp.maximum(i * hb - 1, 0), 0)),
                  pl.BlockSpec((1, HALO, c), lambda bi, i: (bi, jnp.minimum((i + 1) * hb, n_hb - 1), 0)),
                  pl.BlockSpec(pool_w.shape, lambda bi, i: (0, 0, 0)),
                  pl.BlockSpec((1, c), lambda bi, i: (0, 0))],
        out_specs=pl.BlockSpec((1, tm, c), lambda bi, i: (bi, i, 0)),
        out_shape=jax.ShapeDtypeStruct((b, s, c), BF16),
        compiler_params=_params("arbitrary", "arbitrary"),
        name="pool",
    )(f_pool, f_pool, f_pool, pool_w, pool_scale.reshape(1, c))


def _merge_kernel(h_ref, cy_ref, ay_ref, py_ref, wg_ref, wc_ref, wa_ref, wp_ref, o_ref):
    h = h_ref[...]
    out = _sigmoid(_dot(h, wg_ref[0])) * _dot(cy_ref[...], wc_ref[...])
    out += _sigmoid(_dot(h, wg_ref[1])) * _dot(ay_ref[...], wa_ref[...])
    out += _sigmoid(_dot(h, wg_ref[2])) * _dot(py_ref[...], wp_ref[...])
    o_ref[...] = out.astype(o_ref.dtype)


def _merge(h, conv_y, attn_y, pool_y, w_gate, w_conv_out, w_mla_out, w_pool_out, tm, tn):
    m, d = h.shape
    n = w_conv_out.shape[1]
    row = lambda a: pl.BlockSpec((tm, a.shape[1]), lambda i, j: (i, 0))
    col = lambda w: pl.BlockSpec((w.shape[0], tn), lambda i, j: (0, j))
    return pl.pallas_call(
        _merge_kernel,
        grid=(m // tm, n // tn),
        in_specs=[row(h), row(conv_y), row(attn_y), row(pool_y),
                  pl.BlockSpec((N_BRANCHES, d, tn), lambda i, j: (0, 0, j)),
                  col(w_conv_out), col(w_mla_out), col(w_pool_out)],
        out_specs=pl.BlockSpec((tm, tn), lambda i, j: (i, j)),
        out_shape=jax.ShapeDtypeStruct((m, n), BF16),
        compiler_params=_params("arbitrary", "arbitrary"),
        name="merge",
    )(h, conv_y, attn_y, pool_y, w_gate, w_conv_out, w_mla_out, w_pool_out)


def _mm_resid_kernel(a_ref, w_ref, x_ref, g_ref, o_ref):
    o_ref[0] = x_ref[0] + g_ref[0] * _dot(a_ref[0], w_ref[...])


def _matmul_resid(a, w, x, g, tm, tn, name):
    b, s, k = a.shape
    n = w.shape[1]
    return pl.pallas_call(
        _mm_resid_kernel,
        grid=(b, s // tm, n // tn),
        in_specs=[pl.BlockSpec((1, tm, k), lambda bi, i, j: (bi, i, 0)),
                  pl.BlockSpec((k, tn), lambda bi, i, j: (0, j)),
                  pl.BlockSpec((1, tm, tn), lambda bi, i, j: (bi, i, j)),
                  pl.BlockSpec((1, 1, tn), lambda bi, i, j: (bi, 0, j))],
        out_specs=pl.BlockSpec((1, tm, tn), lambda bi, i, j: (bi, i, j)),
        out_shape=jax.ShapeDtypeStruct((b, s, n), F32),
        compiler_params=_params("arbitrary", "arbitrary", "arbitrary"),
        name=name,
    )(a, w, x, g)


def _ffn_up_kernel(a_ref, wg_ref, wu_ref, o_ref):
    a = a_ref[...]
    gate = _dot(a, wg_ref[...])
    o_ref[...] = (gate * _sigmoid(gate) * _dot(a, wu_ref[...])).astype(o_ref.dtype)


def _ffn_up(h, w_gate, w_up, tm, tn):
    m, k = h.shape
    n = w_gate.shape[1]
    wspec = pl.BlockSpec((k, tn), lambda i, j: (0, j))
    return pl.pallas_call(
        _ffn_up_kernel,
        grid=(m // tm, n // tn),
        in_specs=[pl.BlockSpec((tm, k), lambda i, j: (i, 0)), wspec, wspec],
        out_specs=pl.BlockSpec((tm, tn), lambda i, j: (i, j)),
        out_shape=jax.ShapeDtypeStruct((m, n), BF16),
        compiler_params=_params("arbitrary", "arbitrary"),
        name="ffn_up",
    )(h, w_gate, w_up)


def _rope_tables(n_tokens, rotate):
    lane = jnp.arange(LANES)
    valid = lane < QK_ROPE_DIM
    if not rotate:
        cos = jnp.broadcast_to(valid.astype(F32), (n_tokens, LANES))
        zero = jnp.zeros((n_tokens, LANES), F32)
        return cos, zero, zero
    axis_dim = QK_ROPE_DIM // 2
    t = jnp.arange(n_tokens, dtype=jnp.int32)
    pos = jnp.stack([t // GRID_W, t % GRID_W], axis=1).astype(F32)
    inv_freq = 1.0 / (ROPE_THETA ** (jnp.arange(0, axis_dim, 2, dtype=F32) / axis_dim))
    axis = jnp.clip(lane // axis_dim, 0, 1)
    upper = (lane % axis_dim) >= axis_dim // 2
    ang = pos[:, axis] * inv_freq[lane % (axis_dim // 2)][None, :]
    cos = jnp.where(valid, jnp.cos(ang), 0.0)
    sin = jnp.sin(ang)
    sin_lo = jnp.where(valid & ~upper, -sin, 0.0)
    sin_hi = jnp.where(valid & upper, sin, 0.0)
    return cos, sin_lo, sin_hi


def _layer_weights(l, w_in, w_uq, w_ukv, w_conv_out, w_mla_out, w_pool_out, w_o, pool_w,
                   w_ffn_gate, w_ffn_up, w_ffn_down):
    d = D_MODEL
    wi = w_in[l]
    w_kv = jnp.pad(wi[:, COL_KV:COL_POOL], ((0, 0), (0, LANES - QK_ROPE_DIM)))
    wq = w_uq[l].reshape(Q_LORA_RANK, MLA_HEADS, QK_NOPE_DIM + QK_ROPE_DIM)
    wq = jnp.pad(wq, ((0, 0), (0, 0), (0, QK_PAD - QK_NOPE_DIM - QK_ROPE_DIM))).transpose(1, 0, 2)
    return dict(
        conv=wi[:, :COL_Q].astype(BF16),
        cq=wi[:, COL_Q:COL_KV].astype(BF16),
        ckv=w_kv.astype(BF16),
        pool=wi[:, COL_POOL:COL_GATE].astype(BF16),
        gate=wi[:, COL_GATE:].reshape(d, N_BRANCHES, d).transpose(1, 0, 2).astype(BF16),
        uq=wq.astype(BF16),
        ukv=w_ukv[l].astype(BF16),
        conv_out=w_conv_out[l].astype(BF16),
        mla_out=w_mla_out[l].astype(BF16),
        pool_out=w_pool_out[l].astype(BF16),
        o=w_o[l].astype(BF16),
        pool_w=pool_w[l].astype(BF16),
        ffn_gate=w_ffn_gate[l].astype(BF16),
        ffn_up=w_ffn_up[l].astype(BF16),
        ffn_down=w_ffn_down[l].astype(BF16),
    )


def _kv_path(h2d, b, s, w, kv_norm_w, rope, tm):
    ckvn, kpe = _ckv_proj(h2d, w["ckv"], kv_norm_w, rope, tm)
    return _kv_up(ckvn.reshape(b, s, -1), kpe.reshape(b, s, -1), w["ukv"], tm)


def kernel(x, c, ctx, c_ctx, w_mod, b_mod, norm_mix_w, norm_ffn_w, w_in, conv_w, w_conv_out,
           q_norm_w, kv_norm_w, w_uq, w_ukv, w_mla_out, pool_w, pool_scale, w_pool_out, w_o,
           w_ffn_gate, w_ffn_up, w_ffn_down, final_norm_w):
    b, n_lat, d = x.shape
    n_ctx = ctx.shape[1]
    rope_lat = _rope_tables(n_lat, True)
    rope_ctx = _rope_tables(n_ctx, False)

    cvec = jnp.concatenate([c, c_ctx[None, :], jnp.zeros((8 - b - 1, d), F32)], axis=0)
    mod_all = _mod(cvec, w_mod, b_mod).reshape(DEPTH, 8, N_MOD, d)

    tm_lat = 512
    tm_ctx = n_ctx
    h_ctx = ctx
    for l in range(DEPTH):
        last = l == DEPTH - 1
        w = _layer_weights(l, w_in, w_uq, w_ukv, w_conv_out, w_mla_out, w_pool_out, w_o, pool_w,
                           w_ffn_gate, w_ffn_up, w_ffn_down)
        mod_lat = [mod_all[l, :b, i].reshape(b, 1, d) for i in range(N_MOD)]
        mod_ctx = [jnp.broadcast_to(mod_all[l, b, i].reshape(1, 1, d), (b, 1, d)) for i in range(N_MOD)]

        def mix(stream, mod, s, rope, tm, kv_extra):
            sh1, sc1, g1, sh2, sc2, g2 = mod
            h = _norm_mod(stream, norm_mix_w[l], sh1, sc1, tm)
            h2d = h.reshape(b * s, d)
            k_own, v_own = _kv_path(h2d, b, s, w, kv_norm_w[l], rope, tm)
            cqn = _cq_proj(h2d, w["cq"], q_norm_w[l], tm)
            q = _q_up(cqn.reshape(b, s, -1), w["uq"], rope, tm)
            if kv_extra is None:
                attn = _flash_ctx(q, k_own, v_own)
            else:
                attn = _flash(q, kv_extra[0], kv_extra[1], k_own, v_own, 512, 512)
            f_conv = _matmul(h2d, w["conv"], tm, 1024, "in_conv")
            conv_y = _conv_branch(f_conv.reshape(b, s, -1), conv_w[l], tm)
            f_pool = _matmul(h2d, w["pool"], tm, 1024, "in_pool")
            pool_y = _pool_branch(f_pool.reshape(b, s, -1), w["pool_w"], pool_scale[l], tm)
            merged = _merge(h2d, conv_y.reshape(b * s, -1), attn.reshape(b * s, -1),
                            pool_y.reshape(b * s, -1), w["gate"], w["conv_out"], w["mla_out"],
                            w["pool_out"], tm, 256)
            x1 = _matmul_resid(merged.reshape(b, s, d), w["o"], stream, g1, tm, 512, "w_o")
            h_ffn = _norm_mod(x1, norm_ffn_w[l], sh2, sc2, tm)
            hid = _ffn_up(h_ffn.reshape(b * s, d), w["ffn_gate"], w["ffn_up"], tm, 256)
            x2 = _matmul_resid(hid.reshape(b, s, -1), w["ffn_down"], x1, g2, tm, 512, "ffn_down")
            return x2, (k_own, v_own)

        if last:
            hc = _norm_mod(h_ctx, norm_mix_w[l], mod_ctx[0], mod_ctx[1], tm_ctx)
            kv_c = _kv_path(hc.reshape(b * n_ctx, d), b, n_ctx, w, kv_norm_w[l], rope_ctx, tm_ctx)
        else:
            h_ctx, kv_c = mix(h_ctx, mod_ctx, n_ctx, rope_ctx, tm_ctx, None)
        x, _ = mix(x, mod_lat, n_lat, rope_lat, tm_lat, kv_c)

    return _final_norm(x.reshape(b * n_lat, d), final_norm_w, 512).reshape(b, n_lat, d)
```

```python
import functools

import jax
import jax.numpy as jnp
from jax import lax
from jax.experimental import pallas as pl
from jax.experimental.pallas import tpu as pltpu

F32 = jnp.float32
BF16 = jnp.bfloat16

D_MODEL = 4096
DEPTH = 2
GRID_W = 64
CONV_DIM = 1024
MLA_HEADS = 16
QK_NOPE_DIM = 128
QK_ROPE_DIM = 64
V_HEAD_DIM = 128
Q_LORA_RANK = 1024
KV_LORA_RANK = 512
ROPE_THETA = 10000.0
POOL_WINDOWS = (2, 4, 8, 16)
POOL_GROUP_DIM = 256
POOL_DIM = len(POOL_WINDOWS) * POOL_GROUP_DIM
N_BRANCHES = 3
N_MOD = 6
NORM_EPS = 1e-6

COL_Q = 3 * CONV_DIM
COL_KV = COL_Q + Q_LORA_RANK
COL_KPE = COL_KV + KV_LORA_RANK
COL_POOL = COL_KPE + QK_ROPE_DIM
COL_GATE = COL_POOL + POOL_DIM

LANES = 128
QK_PAD = 2 * LANES
HALO = 16
VMEM_LIMIT = 56 * 1024 * 1024
V_PAD = 2 * LANES
FLASH_ROW_GROUPS = 4
Q_SCALE =(QK_NOPE_DIM + QK_ROPE_DIM) ** -0.5 * 1.4426950408889634


def _params(*sem):
    return pltpu.CompilerParams(dimension_semantics=sem, vmem_limit_bytes=VMEM_LIMIT)


def _sigmoid(x):
    return 1.0 / (1.0 + jnp.exp(-x))


def _dot(a, b):
    return jnp.dot(a, b, preferred_element_type=F32)


def _rms(x, w):
    ms = jnp.mean(x * x, axis=-1, keepdims=True)
    return x * lax.rsqrt(ms + NORM_EPS) * w


def _rope(pe, cos, sin_lo, sin_hi):
    return (pe * cos + pltpu.roll(pe, LANES - 16, 1) * sin_lo
            + pltpu.roll(pe, 16, 1) * sin_hi)


def _mod_kernel(c_ref, w_ref, b_ref, o_ref):
    c = c_ref[...]
    a = (c * _sigmoid(c)).astype(BF16)
    o_ref[0] = _dot(a, w_ref[0].astype(BF16)) + b_ref[0]


def _mod(cvec, w_mod, b_mod):
    depth, d, n = w_mod.shape
    rows = cvec.shape[0]
    tn = 512
    return pl.pallas_call(
        _mod_kernel,
        grid=(depth, n // tn),
        in_specs=[pl.BlockSpec((rows, d), lambda l, j: (0, 0)),
                  pl.BlockSpec((1, d, tn), lambda l, j: (l, 0, j)),
                  pl.BlockSpec((1, 1, tn), lambda l, j: (l, 0, j))],
        out_specs=pl.BlockSpec((1, rows, tn), lambda l, j: (l, 0, j)),
        out_shape=jax.ShapeDtypeStruct((depth, rows, n), F32),
        compiler_params=_params("arbitrary", "arbitrary"),
        name="mod",
    )(cvec, w_mod, b_mod.reshape(depth, 1, n))


def _norm_mod_kernel(x_ref, w_ref, sh_ref, sc_ref, o_ref):
    y = _rms(x_ref[0], w_ref[...])
    o_ref[0] = (y * (1.0 + sc_ref[0]) + sh_ref[0]).astype(o_ref.dtype)


def _norm_mod(x, w, shift, scale, tm):
    b, s, d = x.shape
    vec = pl.BlockSpec((1, 1, d), lambda bi, i: (bi, 0, 0))
    return pl.pallas_call(
        _norm_mod_kernel,
        grid=(b, s // tm),
        in_specs=[pl.BlockSpec((1, tm, d), lambda bi, i: (bi, i, 0)),
                  pl.BlockSpec((1, d), lambda bi, i: (0, 0)), vec, vec],
        out_specs=pl.BlockSpec((1, tm, d), lambda bi, i: (bi, i, 0)),
        out_shape=jax.ShapeDtypeStruct((b, s, d), BF16),
        compiler_params=_params("arbitrary", "arbitrary"),
        name="norm_mod",
    )(x, w.reshape(1, d), shift, scale)


def _norm_kernel(x_ref, w_ref, o_ref):
    o_ref[...] = _rms(x_ref[...], w_ref[...])


def _final_norm(x, w, tm):
    m, d = x.shape
    return pl.pallas_call(
        _norm_kernel,
        grid=(m // tm,),
        in_specs=[pl.BlockSpec((tm, d), lambda i: (i, 0)),
                  pl.BlockSpec((1, d), lambda i: (0, 0))],
        out_specs=pl.BlockSpec((tm, d), lambda i: (i, 0)),
        out_shape=jax.ShapeDtypeStruct((m, d), F32),
        compiler_params=_params("arbitrary"),
        name="final_norm",
    )(x, w.reshape(1, d))


def _mm_kernel(a_ref, w_ref, o_ref):
    o_ref[...] = _dot(a_ref[...], w_ref[...]).astype(o_ref.dtype)


def _matmul(a, w, tm, tn, name):
    m, k = a.shape
    n = w.shape[1]
    return pl.pallas_call(
        _mm_kernel,
        grid=(m // tm, n // tn),
        in_specs=[pl.BlockSpec((tm, k), lambda i, j: (i, 0)),
                  pl.BlockSpec((k, tn), lambda i, j: (0, j))],
        out_specs=pl.BlockSpec((tm, tn), lambda i, j: (i, j)),
        out_shape=jax.ShapeDtypeStruct((m, n), BF16),
        compiler_params=_params("arbitrary", "arbitrary"),
        name=name,
    )(a, w)


def _cq_kernel(a_ref, w_ref, nw_ref, o_ref):
    o_ref[...] = _rms(_dot(a_ref[...], w_ref[...]), nw_ref[...]).astype(o_ref.dtype)


def _cq_proj(h, w, nw, tm):
    m, k = h.shape
    n = w.shape[1]
    return pl.pallas_call(
        _cq_kernel,
        grid=(m // tm,),
        in_specs=[pl.BlockSpec((tm, k), lambda i: (i, 0)),
                  pl.BlockSpec((k, n), lambda i: (0, 0)),
                  pl.BlockSpec((1, n), lambda i: (0, 0))],
        out_specs=pl.BlockSpec((tm, n), lambda i: (i, 0)),
        out_shape=jax.ShapeDtypeStruct((m, n), BF16),
        compiler_params=_params("arbitrary"),
        name="cq_proj",
    )(h, w, nw.reshape(1, n))


def _ckv_kernel(a_ref, w_ref, nw_ref, cos_ref, slo_ref, shi_ref, ckv_ref, kpe_ref):
    r = _dot(a_ref[...], w_ref[...])
    ckv_ref[...] = _rms(r[:, :KV_LORA_RANK], nw_ref[...]).astype(ckv_ref.dtype)
    pe = _rope(r[:, KV_LORA_RANK:], cos_ref[...], slo_ref[...], shi_ref[...])
    kpe_ref[...] = pe.astype(kpe_ref.dtype)


def _ckv_proj(h, w, nw, rope, tm):
    m, k = h.shape
    n = w.shape[1]
    n_pos = rope[0].shape[0] // tm
    tab = pl.BlockSpec((tm, LANES), lambda i: (i % n_pos, 0))
    return pl.pallas_call(
        _ckv_kernel,
        grid=(m // tm,),
        in_specs=[pl.BlockSpec((tm, k), lambda i: (i, 0)),
                  pl.BlockSpec((k, n), lambda i: (0, 0)),
                  pl.BlockSpec((1, KV_LORA_RANK), lambda i: (0, 0)), tab, tab, tab],
        out_specs=[pl.BlockSpec((tm, KV_LORA_RANK), lambda i: (i, 0)),
                   pl.BlockSpec((tm, LANES), lambda i: (i, 0))],
        out_shape=[jax.ShapeDtypeStruct((m, KV_LORA_RANK), BF16),
                   jax.ShapeDtypeStruct((m, LANES), BF16)],
        compiler_params=_params("arbitrary"),
        name="ckv_proj",
    )(h, w, nw.reshape(1, KV_LORA_RANK), *rope)


def _q_up_kernel(a_ref, w_ref, cos_ref, slo_ref, shi_ref, q_ref):
    a = a_ref[0]
    cos, slo, shi = cos_ref[...], slo_ref[...], shi_ref[...]
    for h in range(MLA_HEADS):
        r = _dot(a, w_ref[h])
        q_ref[0, h, :, :LANES] = (r[:, :LANES] * Q_SCALE).astype(q_ref.dtype)
        pe = _rope(r[:, LANES:], cos, slo, shi)
        q_ref[0, h, :, LANES:] = (pe * Q_SCALE).astype(q_ref.dtype)


def _q_up(cqn, w, rope, tm):
    b, s, k = cqn.shape
    tab = pl.BlockSpec((tm, LANES), lambda bi, i: (i, 0))
    return pl.pallas_call(
        _q_up_kernel,
        grid=(b, s // tm),
        in_specs=[pl.BlockSpec((1, tm, k), lambda bi, i: (bi, i, 0)),
                  pl.BlockSpec((MLA_HEADS, k, QK_PAD), lambda bi, i: (0, 0, 0)), tab, tab, tab],
        out_specs=pl.BlockSpec((1, MLA_HEADS, tm, QK_PAD), lambda bi, i: (bi, 0, i, 0)),
        out_shape=jax.ShapeDtypeStruct((b, MLA_HEADS, s, QK_PAD), BF16),
        compiler_params=_params("arbitrary", "arbitrary"),
        name="q_up",
    )(cqn, w, *rope)


def _kv_up_kernel(a_ref, pe_ref, w_ref, k_ref, v_ref):
    a = a_ref[0]
    pe = pe_ref[0]
    width = QK_NOPE_DIM + V_HEAD_DIM
    ones = jnp.ones((a.shape[0], V_PAD - V_HEAD_DIM), v_ref.dtype)
    for h in range(MLA_HEADS):
        r = _dot(a, w_ref[:, h * width:(h + 1) * width])
        k_ref[0, h, :, :LANES] = r[:, :QK_NOPE_DIM].astype(k_ref.dtype)
        k_ref[0, h, :, LANES:] = pe
        v_ref[0, h, :, :V_HEAD_DIM] = r[:, QK_NOPE_DIM:].astype(v_ref.dtype)
        v_ref[0, h, :, V_HEAD_DIM:] = ones


def _kv_up(ckvn, kpe, w, tm):
    b, s, k = ckvn.shape
    return pl.pallas_call(
        _kv_up_kernel,
        grid=(b, s // tm),
        in_specs=[pl.BlockSpec((1, tm, k), lambda bi, i: (bi, i, 0)),
                  pl.BlockSpec((1, tm, LANES), lambda bi, i: (bi, i, 0)),
                  pl.BlockSpec(w.shape, lambda bi, i: (0, 0))],
        out_specs=[pl.BlockSpec((1, MLA_HEADS, tm, QK_PAD), lambda bi, i: (bi, 0, i, 0)),
                   pl.BlockSpec((1, MLA_HEADS, tm, V_PAD), lambda bi, i: (bi, 0, i, 0))],
        out_shape=[jax.ShapeDtypeStruct((b, MLA_HEADS, s, QK_PAD), BF16),
                   jax.ShapeDtypeStruct((b, MLA_HEADS, s, V_PAD), BF16)],
        compiler_params=_params("arbitrary", "arbitrary"),
        name="kv_up",
    )(ckvn, kpe, w)


def _flash_init(m_ref, acc_ref):
    m_ref[...] = jnp.full(m_ref.shape, -jnp.inf, F32)
    acc_ref[...] = jnp.zeros(acc_ref.shape, F32)


def _flash_update(q_ref, k, v, m_ref, acc_ref):
    tq = q_ref.shape[2]
    n_blk = k.shape[0] // LANES
    for g in range(FLASH_ROW_GROUPS):
        rows = slice(g * tq // FLASH_ROW_GROUPS, (g + 1) * tq // FLASH_ROW_GROUPS)
        s = lax.dot_general(q_ref[0, 0, rows, :], k, (((1,), (1,)), ((), ())),
                            preferred_element_type=F32)
        blocks = [s[:, c * LANES:(c + 1) * LANES] for c in range(n_blk)]
        m_blk = functools.reduce(jnp.maximum, blocks)
        m_prev = m_ref[rows, :]
        m_new = jnp.maximum(m_prev, jnp.max(m_blk, axis=1, keepdims=True))
        p = jnp.concatenate([jnp.exp2(blk - m_new).astype(BF16) for blk in blocks], axis=1)
        alpha = jnp.exp2(m_prev - m_new)
        acc_ref[rows, :] = acc_ref[rows, :] * jnp.concatenate([alpha, alpha], axis=1) + _dot(p, v)
        m_ref[rows, :] = m_new


def _flash_finish(o_ref, acc_ref):
    acc = acc_ref[...]
    o_ref[0] = (acc[:, :V_HEAD_DIM] / acc[:, V_HEAD_DIM:]).astype(o_ref.dtype)


def _flash_kernel(q_ref, kc_ref, vc_ref, kl_ref, vl_ref, o_ref, m_ref, acc_ref, *, tk):
    _flash_init(m_ref, acc_ref)
    _flash_update(q_ref, kc_ref[0, 0], vc_ref[0, 0], m_ref, acc_ref)

    for j in range(kl_ref.shape[2] // tk):
        rows = slice(j * tk, (j + 1) * tk)
        _flash_update(q_ref, kl_ref[0, 0, rows, :], vl_ref[0, 0, rows, :], m_ref, acc_ref)
    _flash_finish(o_ref, acc_ref)


def _flash_ctx_kernel(q_ref, kc_ref, vc_ref, o_ref, m_ref, acc_ref):
    _flash_init(m_ref, acc_ref)
    _flash_update(q_ref, kc_ref[0, 0], vc_ref[0, 0], m_ref, acc_ref)
    _flash_finish(o_ref, acc_ref)


def _flash_scratch(tq):
    return [pltpu.VMEM((tq, LANES), F32), pltpu.VMEM((tq, V_PAD), F32)]


def _flash(q, kc, vc, kl, vl, tq, tk):
    b, nh, s, _ = q.shape
    n_ctx = kc.shape[2]
    n_lat = kl.shape[2]
    whole = lambda n, w: pl.BlockSpec((1, 1, n, w), lambda bi, h, i: (bi, h, 0, 0))
    return pl.pallas_call(
        functools.partial(_flash_kernel, tk=tk),
        grid=(b, nh, s // tq),
        in_specs=[pl.BlockSpec((1, 1, tq, QK_PAD), lambda bi, h, i: (bi, h, i, 0)),
                  whole(n_ctx, QK_PAD), whole(n_ctx, V_PAD),
                  whole(n_lat, QK_PAD), whole(n_lat, V_PAD)],
        out_specs=pl.BlockSpec((1, tq, V_HEAD_DIM), lambda bi, h, i: (bi, i, h)),
        out_shape=jax.ShapeDtypeStruct((b, s, nh * V_HEAD_DIM), BF16),
        scratch_shapes=_flash_scratch(tq),
        compiler_params=_params("arbitrary", "arbitrary", "arbitrary"),
        name="flash",
    )(q, kc, vc, kl, vl)


def _flash_ctx(q, kc, vc):
    b, nh, s, _ = q.shape
    blk = lambda w: pl.BlockSpec((1, 1, s, w), lambda bi, h: (bi, h, 0, 0))
    return pl.pallas_call(
        _flash_ctx_kernel,
        grid=(b, nh),
        in_specs=[blk(QK_PAD), blk(QK_PAD), blk(V_PAD)],
        out_specs=pl.BlockSpec((1, s, V_HEAD_DIM), lambda bi, h: (bi, 0, h)),
        out_shape=jax.ShapeDtypeStruct((b, s, nh * V_HEAD_DIM), BF16),
        scratch_shapes=_flash_scratch(s),
        compiler_params=_params("arbitrary", "arbitrary"),
        name="flash_ctx",
    )(q, kc, vc)


def _conv_kernel(bg_ref, cg_ref, xin_ref, cgp_ref, xinp_ref, cgn_ref, xinn_ref, w_ref, o_ref):
    i = pl.program_id(1)
    tm = cg_ref.shape[1]
    u = cg_ref[0].astype(F32) * xin_ref[0].astype(F32)
    u_prev = cgp_ref[0, HALO - 1:HALO, :].astype(F32) * xinp_ref[0, HALO - 1:HALO, :].astype(F32)
    u_next = cgn_ref[0, 0:1, :].astype(F32) * xinn_ref[0, 0:1, :].astype(F32)
    u_prev = jnp.where(i > 0, u_prev, 0.0)
    u_next = jnp.where(i < pl.num_programs(1) - 1, u_next, 0.0)
    row = lax.broadcasted_iota(jnp.int32, u.shape, 0)
    below = jnp.where(row == 0, u_prev, pltpu.roll(u, 1, 0))
    above = jnp.where(row == tm - 1, u_next, pltpu.roll(u, tm - 1, 0))
    y = below * w_ref[0:1, :] + u * w_ref[1:2, :] + above * w_ref[2:3, :]
    o_ref[0] = (bg_ref[0].astype(F32) * y).astype(o_ref.dtype)


def _conv_branch(f_conv, conv_w, tm):
    b, s, _ = f_conv.shape
    c = CONV_DIM
    hb = tm // HALO
    n_hb = s // HALO
    main = lambda col: pl.BlockSpec((1, tm, c), lambda bi, i: (bi, i, col))
    prev = lambda col: pl.BlockSpec((1, HALO, c), lambda bi, i: (bi, jnp.maximum(i * hb - 1, 0), col))
    nxt = lambda col: pl.BlockSpec((1, HALO, c), lambda bi, i: (bi, jnp.minimum((i + 1) * hb, n_hb - 1), col))
    return pl.pallas_call(
        _conv_kernel,
        grid=(b, s // tm),
        in_specs=[main(0), main(1), main(2), prev(1), prev(2), nxt(1), nxt(2),
                  pl.BlockSpec(conv_w.shape, lambda bi, i: (0, 0))],
        out_specs=pl.BlockSpec((1, tm, c), lambda bi, i: (bi, i, 0)),
        out_shape=jax.ShapeDtypeStruct((b, s, c), BF16),
        compiler_params=_params("arbitrary", "arbitrary"),
        name="conv",
    )(f_conv, f_conv, f_conv, f_conv, f_conv, f_conv, f_conv, conv_w)


def _pool_kernel(u_ref, up_ref, un_ref, w_ref, sc_ref, o_ref, *, seq_len):
    i = pl.program_id(1)
    tm = u_ref.shape[1]
    rows = tm + 2 * HALO
    prev = jnp.where(i > 0, up_ref[0].astype(F32), 0.0)
    nxt = jnp.where(i < pl.num_programs(1) - 1, un_ref[0].astype(F32), 0.0)
    ext = jnp.concatenate([prev, u_ref[0].astype(F32), nxt], axis=0)
    t = i * tm + lax.broadcasted_iota(jnp.int32, (tm, 1), 0)
    g = POOL_GROUP_DIM
    for gi, win in enumerate(POOL_WINDOWS):
        e = ext[:, gi * g:(gi + 1) * g]
        acc = e + pltpu.roll(e, 1, 0)
        half = 1
        while 2 * half < win:
            acc = pltpu.roll(acc, half, 0) + pltpu.roll(acc, rows - half, 0)
            half *= 2
        lo = jnp.maximum(t - win // 2, 0)
        hi = jnp.minimum(t + (win - win // 2), seq_len)
        cnt = (hi - lo).astype(F32)
        p = acc[HALO:HALO + tm] / cnt - e[HALO:HALO + tm]
        y = _dot(p.astype(BF16), w_ref[gi]) * sc_ref[:, gi * g:(gi + 1) * g]
        o_ref[0, :, gi * g:(gi + 1) * g] = y.astype(o_ref.dtype)


def _pool_branch(f_pool, pool_w, pool_scale, tm):
    b, s, c = f_pool.shape
    hb = tm // HALO
    n_hb = s // HALO
    return pl.pallas_call(
        functools.partial(_pool_kernel, seq_len=s),
        grid=(b, s // tm),
        in_specs=[pl.BlockSpec((1, tm, c), lambda bi, i: (bi, i, 0)),
                  pl.BlockSpec((1, HALO, c), lambda bi, i: (bi, jnp.maximum(i * hb - 1, 0), 0)),
                  pl.BlockSpec((1, HALO, c), lambda bi, i: (bi, jnp.minimum((i + 1) * hb, n_hb - 1), 0)),
                  pl.BlockSpec(pool_w.shape, lambda bi, i: (0, 0, 0)),
                  pl.BlockSpec((1, c), lambda bi, i: (0, 0))],
        out_specs=pl.BlockSpec((1, tm, c), lambda bi, i: (bi, i, 0)),
        out_shape=jax.ShapeDtypeStruct((b, s, c), BF16),
        compiler_params=_params("arbitrary", "arbitrary"),
        name="pool",
    )(f_pool, f_pool, f_pool, pool_w, pool_scale.reshape(1, c))


def _merge_kernel(h_ref, cy_ref, ay_ref, py_ref, wg_ref, wc_ref, wa_ref, wp_ref, o_ref):
    h = h_ref[...]
    out = _sigmoid(_dot(h, wg_ref[0])) * _dot(cy_ref[...], wc_ref[...])
    out += _sigmoid(_dot(h, wg_ref[1])) * _dot(ay_ref[...], wa_ref[...])
    out += _sigmoid(_dot(h, wg_ref[2])) * _dot(py_ref[...], wp_ref[...])
    o_ref[...] = out.astype(o_ref.dtype)


def _merge(h, conv_y, attn_y, pool_y, w_gate, w_conv_out, w_mla_out, w_pool_out, tm, tn):
    m, d = h.shape
    n = w_conv_out.shape[1]
    row = lambda a: pl.BlockSpec((tm, a.shape[1]), lambda i, j: (i, 0))
    col = lambda w: pl.BlockSpec((w.shape[0], tn), lambda i, j: (0, j))
    return pl.pallas_call(
        _merge_kernel,
        grid=(m // tm, n // tn),
        in_specs=[row(h), row(conv_y), row(attn_y), row(pool_y),
                  pl.BlockSpec((N_BRANCHES, d, tn), lambda i, j: (0, 0, j)),
                  col(w_conv_out), col(w_mla_out), col(w_pool_out)],
        out_specs=pl.BlockSpec((tm, tn), lambda i, j: (i, j)),
        out_shape=jax.ShapeDtypeStruct((m, n), BF16),
        compiler_params=_params("arbitrary", "arbitrary"),
        name="merge",
    )(h, conv_y, attn_y, pool_y, w_gate, w_conv_out, w_mla_out, w_pool_out)


def _mm_resid_kernel(a_ref, w_ref, x_ref, g_ref, o_ref):
    o_ref[0] = x_ref[0] + g_ref[0] * _dot(a_ref[0], w_ref[...])


def _matmul_resid(a, w, x, g, tm, tn, name):
    b, s, k = a.shape
    n = w.shape[1]
    return pl.pallas_call(
        _mm_resid_kernel,
        grid=(b, s // tm, n // tn),
        in_specs=[pl.BlockSpec((1, tm, k), lambda bi, i, j: (bi, i, 0)),
                  pl.BlockSpec((k, tn), lambda bi, i, j: (0, j)),
                  pl.BlockSpec((1, tm, tn), lambda bi, i, j: (bi, i, j)),
                  pl.BlockSpec((1, 1, tn), lambda bi, i, j: (bi, 0, j))],
        out_specs=pl.BlockSpec((1, tm, tn), lambda bi, i, j: (bi, i, j)),
        out_shape=jax.ShapeDtypeStruct((b, s, n), F32),
        compiler_params=_params("arbitrary", "arbitrary", "arbitrary"),
        name=name,
    )(a, w, x, g)


def _ffn_up_kernel(a_ref, wg_ref, wu_ref, o_ref):
    a = a_ref[...]
    gate = _dot(a, wg_ref[...])
    o_ref[...] = (gate * _sigmoid(gate) * _dot(a, wu_ref[...])).astype(o_ref.dtype)


def _ffn_up(h, w_gate, w_up, tm, tn):
    m, k = h.shape
    n = w_gate.shape[1]
    wspec = pl.BlockSpec((k, tn), lambda i, j: (0, j))
    return pl.pallas_call(
        _ffn_up_kernel,
        grid=(m // tm, n // tn),
        in_specs=[pl.BlockSpec((tm, k), lambda i, j: (i, 0)), wspec, wspec],
        out_specs=pl.BlockSpec((tm, tn), lambda i, j: (i, j)),
        out_shape=jax.ShapeDtypeStruct((m, n), BF16),
        compiler_params=_params("arbitrary", "arbitrary"),
        name="ffn_up",
    )(h, w_gate, w_up)


def _rope_tables(n_tokens, rotate):
    lane = jnp.arange(LANES)
    valid = lane < QK_ROPE_DIM
    if not rotate:
        cos = jnp.broadcast_to(valid.astype(F32), (n_tokens, LANES))
        zero = jnp.zeros((n_tokens, LANES), F32)
        return cos, zero, zero
    axis_dim = QK_ROPE_DIM // 2
    t = jnp.arange(n_tokens, dtype=jnp.int32)
    pos = jnp.stack([t // GRID_W, t % GRID_W], axis=1).astype(F32)
    inv_freq = 1.0 / (ROPE_THETA ** (jnp.arange(0, axis_dim, 2, dtype=F32) / axis_dim))
    axis = jnp.clip(lane // axis_dim, 0, 1)
    upper = (lane % axis_dim) >= axis_dim // 2
    ang = pos[:, axis] * inv_freq[lane % (axis_dim // 2)][None, :]
    cos = jnp.where(valid, jnp.cos(ang), 0.0)
    sin = jnp.sin(ang)
    sin_lo = jnp.where(valid & ~upper, -sin, 0.0)
    sin_hi = jnp.where(valid & upper, sin, 0.0)
    return cos, sin_lo, sin_hi


def _layer_weights(l, w_in, w_uq, w_ukv, w_conv_out, w_mla_out, w_pool_out, w_o, pool_w,
                   w_ffn_gate, w_ffn_up, w_ffn_down):
    d = D_MODEL
    wi = w_in[l]
    w_kv = jnp.pad(wi[:, COL_KV:COL_POOL], ((0, 0), (0, LANES - QK_ROPE_DIM)))
    wq = w_uq[l].reshape(Q_LORA_RANK, MLA_HEADS, QK_NOPE_DIM + QK_ROPE_DIM)
    wq = jnp.pad(wq, ((0, 0), (0, 0), (0, QK_PAD - QK_NOPE_DIM - QK_ROPE_DIM))).transpose(1, 0, 2)
    return dict(
        conv=wi[:, :COL_Q].astype(BF16),
        cq=wi[:, COL_Q:COL_KV].astype(BF16),
        ckv=w_kv.astype(BF16),
        pool=wi[:, COL_POOL:COL_GATE].astype(BF16),
        gate=wi[:, COL_GATE:].reshape(d, N_BRANCHES, d).transpose(1, 0, 2).astype(BF16),
        uq=wq.astype(BF16),
        ukv=w_ukv[l].astype(BF16),
        conv_out=w_conv_out[l].astype(BF16),
        mla_out=w_mla_out[l].astype(BF16),
        pool_out=w_pool_out[l].astype(BF16),
        o=w_o[l].astype(BF16),
        pool_w=pool_w[l].astype(BF16),
        ffn_gate=w_ffn_gate[l].astype(BF16),
        ffn_up=w_ffn_up[l].astype(BF16),
        ffn_down=w_ffn_down[l].astype(BF16),
    )


def _kv_path(h2d, b, s, w, kv_norm_w, rope, tm):
    ckvn, kpe = _ckv_proj(h2d, w["ckv"], kv_norm_w, rope, tm)
    return _kv_up(ckvn.reshape(b, s, -1), kpe.reshape(b, s, -1), w["ukv"], tm)


def kernel(x, c, ctx, c_ctx, w_mod, b_mod, norm_mix_w, norm_ffn_w, w_in, conv_w, w_conv_out,
           q_norm_w, kv_norm_w, w_uq, w_ukv, w_mla_out, pool_w, pool_scale, w_pool_out, w_o,
           w_ffn_gate, w_ffn_up, w_ffn_down, final_norm_w):
    b, n_lat, d = x.shape
    n_ctx = ctx.shape[1]
    rope_lat = _rope_tables(n_lat, True)
    rope_ctx = _rope_tables(n_ctx, False)

    cvec = jnp.concatenate([c, c_ctx[None, :], jnp.zeros((8 - b - 1, d), F32)], axis=0)
    mod_all = _mod(cvec, w_mod, b_mod).reshape(DEPTH, 8, N_MOD, d)

    tm_lat = 512
    tm_ctx = n_ctx
    h_ctx = ctx
    for l in range(DEPTH):
        last = l == DEPTH - 1
        w = _layer_weights(l, w_in, w_uq, w_ukv, w_conv_out, w_mla_out, w_pool_out, w_o, pool_w,
                           w_ffn_gate, w_ffn_up, w_ffn_down)
        mod_lat = [mod_all[l, :b, i].reshape(b, 1, d) for i in range(N_MOD)]
        mod_ctx = [jnp.broadcast_to(mod_all[l, b, i].reshape(1, 1, d), (b, 1, d)) for i in range(N_MOD)]

        def mix(stream, mod, s, rope, tm, kv_extra):
            sh1, sc1, g1, sh2, sc2, g2 = mod
            h = _norm_mod(stream, norm_mix_w[l], sh1, sc1, tm)
            h2d = h.reshape(b * s, d)
            k_own, v_own = _kv_path(h2d, b, s, w, kv_norm_w[l], rope, tm)
            cqn = _cq_proj(h2d, w["cq"], q_norm_w[l], tm)
            q = _q_up(cqn.reshape(b, s, -1), w["uq"], rope, tm)
            if kv_extra is None:
                attn = _flash_ctx(q, k_own, v_own)
            else:
                attn = _flash(q, kv_extra[0], kv_extra[1], k_own, v_own, 1024, 2048)
            f_conv = _matmul(h2d, w["conv"], tm, 1024, "in_conv")
            conv_y = _conv_branch(f_conv.reshape(b, s, -1), conv_w[l], tm)
            f_pool = _matmul(h2d, w["pool"], tm, 1024, "in_pool")
            pool_y = _pool_branch(f_pool.reshape(b, s, -1), w["pool_w"], pool_scale[l], tm)
            merged = _merge(h2d, conv_y.reshape(b * s, -1), attn.reshape(b * s, -1),
                            pool_y.reshape(b * s, -1), w["gate"], w["conv_out"], w["mla_out"],
                            w["pool_out"], tm, 256)
            x1 = _matmul_resid(merged.reshape(b, s, d), w["o"], stream, g1, tm, 512, "w_o")
            h_ffn = _norm_mod(x1, norm_ffn_w[l], sh2, sc2, tm)
            hid = _ffn_up(h_ffn.reshape(b * s, d), w["ffn_gate"], w["ffn_up"], tm, 256)
            x2 = _matmul_resid(hid.reshape(b, s, -1), w["ffn_down"], x1, g2, tm, 512, "ffn_down")
            return x2, (k_own, v_own)

        if last:
            hc = _norm_mod(h_ctx, norm_mix_w[l], mod_ctx[0], mod_ctx[1], tm_ctx)
            kv_c = _kv_path(hc.reshape(b * n_ctx, d), b, n_ctx, w, kv_norm_w[l], rope_ctx, tm_ctx)
        else:
            h_ctx, kv_c = mix(h_ctx, mod_ctx, n_ctx, rope_ctx, tm_ctx, None)
        x, _ = mix(x, mod_lat, n_lat, rope_lat, tm_lat, kv_c)

    return _final_norm(x.reshape(b * n_lat, d), final_norm_w, 512).reshape(b, n_lat, d)
```

```python
import functools

import jax
import jax.numpy as jnp
from jax import lax
from jax.experimental import pallas as pl
from jax.experimental.pallas import tpu as pltpu

F32 = jnp.float32
BF16 = jnp.bfloat16

D_MODEL = 4096
DEPTH = 2
GRID_W = 64
CONV_DIM = 1024
MLA_HEADS = 16
QK_NOPE_DIM = 128
QK_ROPE_DIM = 64
V_HEAD_DIM = 128
Q_LORA_RANK = 1024
KV_LORA_RANK = 512
ROPE_THETA = 10000.0
POOL_WINDOWS = (2, 4, 8, 16)
POOL_GROUP_DIM = 256
POOL_DIM = len(POOL_WINDOWS) * POOL_GROUP_DIM
N_BRANCHES = 3
N_MOD = 6
NORM_EPS = 1e-6

COL_Q = 3 * CONV_DIM
COL_KV = COL_Q + Q_LORA_RANK
COL_KPE = COL_KV + KV_LORA_RANK
COL_POOL = COL_KPE + QK_ROPE_DIM
COL_GATE = COL_POOL + POOL_DIM

LANES = 128
QK_PAD = 2 * LANES
HALO = 16
VMEM_LIMIT = 56 * 1024 * 1024
V_PAD = 2 * LANES
FLASH_GROUP_ROWS = 256
Q_SCALE =(QK_NOPE_DIM + QK_ROPE_DIM) ** -0.5 * 1.4426950408889634


def _params(*sem):
    return pltpu.CompilerParams(dimension_semantics=sem, vmem_limit_bytes=VMEM_LIMIT)


def _sigmoid(x):
    return 1.0 / (1.0 + jnp.exp(-x))


def _dot(a, b):
    return jnp.dot(a, b, preferred_element_type=F32)


def _rms(x, w):
    ms = jnp.mean(x * x, axis=-1, keepdims=True)
    return x * lax.rsqrt(ms + NORM_EPS) * w


def _rope(pe, cos, sin_lo, sin_hi):
    return (pe * cos + pltpu.roll(pe, LANES - 16, 1) * sin_lo
            + pltpu.roll(pe, 16, 1) * sin_hi)


def _mod_kernel(c_ref, w_ref, b_ref, o_ref):
    c = c_ref[...]
    a = (c * _sigmoid(c)).astype(BF16)
    o_ref[0] = _dot(a, w_ref[0].astype(BF16)) + b_ref[0]


def _mod(cvec, w_mod, b_mod):
    depth, d, n = w_mod.shape
    rows = cvec.shape[0]
    tn = 512
    return pl.pallas_call(
        _mod_kernel,
        grid=(depth, n // tn),
        in_specs=[pl.BlockSpec((rows, d), lambda l, j: (0, 0)),
                  pl.BlockSpec((1, d, tn), lambda l, j: (l, 0, j)),
                  pl.BlockSpec((1, 1, tn), lambda l, j: (l, 0, j))],
        out_specs=pl.BlockSpec((1, rows, tn), lambda l, j: (l, 0, j)),
        out_shape=jax.ShapeDtypeStruct((depth, rows, n), F32),
        compiler_params=_params("arbitrary", "arbitrary"),
        name="mod",
    )(cvec, w_mod, b_mod.reshape(depth, 1, n))


def _norm_mod_kernel(x_ref, w_ref, sh_ref, sc_ref, o_ref):
    y = _rms(x_ref[0], w_ref[...])
    o_ref[0] = (y * (1.0 + sc_ref[0]) + sh_ref[0]).astype(o_ref.dtype)


def _norm_mod(x, w, shift, scale, tm):
    b, s, d = x.shape
    vec = pl.BlockSpec((1, 1, d), lambda bi, i: (bi, 0, 0))
    return pl.pallas_call(
        _norm_mod_kernel,
        grid=(b, s // tm),
        in_specs=[pl.BlockSpec((1, tm, d), lambda bi, i: (bi, i, 0)),
                  pl.BlockSpec((1, d), lambda bi, i: (0, 0)), vec, vec],
        out_specs=pl.BlockSpec((1, tm, d), lambda bi, i: (bi, i, 0)),
        out_shape=jax.ShapeDtypeStruct((b, s, d), BF16),
        compiler_params=_params("arbitrary", "arbitrary"),
        name="norm_mod",
    )(x, w.reshape(1, d), shift, scale)


def _norm_kernel(x_ref, w_ref, o_ref):
    o_ref[...] = _rms(x_ref[...], w_ref[...])


def _final_norm(x, w, tm):
    m, d = x.shape
    return pl.pallas_call(
        _norm_kernel,
        grid=(m // tm,),
        in_specs=[pl.BlockSpec((tm, d), lambda i: (i, 0)),
                  pl.BlockSpec((1, d), lambda i: (0, 0))],
        out_specs=pl.BlockSpec((tm, d), lambda i: (i, 0)),
        out_shape=jax.ShapeDtypeStruct((m, d), F32),
        compiler_params=_params("arbitrary"),
        name="final_norm",
    )(x, w.reshape(1, d))


def _mm_kernel(a_ref, w_ref, o_ref):
    o_ref[...] = _dot(a_ref[...], w_ref[...]).astype(o_ref.dtype)


def _matmul(a, w, tm, tn, name):
    m, k = a.shape
    n = w.shape[1]
    return pl.pallas_call(
        _mm_kernel,
        grid=(m // tm, n // tn),
        in_specs=[pl.BlockSpec((tm, k), lambda i, j: (i, 0)),
                  pl.BlockSpec((k, tn), lambda i, j: (0, j))],
        out_specs=pl.BlockSpec((tm, tn), lambda i, j: (i, j)),
        out_shape=jax.ShapeDtypeStruct((m, n), BF16),
        compiler_params=_params("arbitrary", "arbitrary"),
        name=name,
    )(a, w)


def _cq_kernel(a_ref, w_ref, nw_ref, o_ref):
    o_ref[...] = _rms(_dot(a_ref[...], w_ref[...]), nw_ref[...]).astype(o_ref.dtype)


def _cq_proj(h, w, nw, tm):
    m, k = h.shape
    n = w.shape[1]
    return pl.pallas_call(
        _cq_kernel,
        grid=(m // tm,),
        in_specs=[pl.BlockSpec((tm, k), lambda i: (i, 0)),
                  pl.BlockSpec((k, n), lambda i: (0, 0)),
                  pl.BlockSpec((1, n), lambda i: (0, 0))],
        out_specs=pl.BlockSpec((tm, n), lambda i: (i, 0)),
        out_shape=jax.ShapeDtypeStruct((m, n), BF16),
        compiler_params=_params("arbitrary"),
        name="cq_proj",
    )(h, w, nw.reshape(1, n))


def _ckv_kernel(a_ref, w_ref, nw_ref, cos_ref, slo_ref, shi_ref, ckv_ref, kpe_ref):
    r = _dot(a_ref[...], w_ref[...])
    ckv_ref[...] = _rms(r[:, :KV_LORA_RANK], nw_ref[...]).astype(ckv_ref.dtype)
    pe = _rope(r[:, KV_LORA_RANK:], cos_ref[...], slo_ref[...], shi_ref[...])
    kpe_ref[...] = pe.astype(kpe_ref.dtype)


def _ckv_proj(h, w, nw, rope, tm):
    m, k = h.shape
    n = w.shape[1]
    n_pos = rope[0].shape[0] // tm
    tab = pl.BlockSpec((tm, LANES), lambda i: (i % n_pos, 0))
    return pl.pallas_call(
        _ckv_kernel,
        grid=(m // tm,),
        in_specs=[pl.BlockSpec((tm, k), lambda i: (i, 0)),
                  pl.BlockSpec((k, n), lambda i: (0, 0)),
                  pl.BlockSpec((1, KV_LORA_RANK), lambda i: (0, 0)), tab, tab, tab],
        out_specs=[pl.BlockSpec((tm, KV_LORA_RANK), lambda i: (i, 0)),
                   pl.BlockSpec((tm, LANES), lambda i: (i, 0))],
        out_shape=[jax.ShapeDtypeStruct((m, KV_LORA_RANK), BF16),
                   jax.ShapeDtypeStruct((m, LANES), BF16)],
        compiler_params=_params("arbitrary"),
        name="ckv_proj",
    )(h, w, nw.reshape(1, KV_LORA_RANK), *rope)


def _q_up_kernel(a_ref, w_ref, cos_ref, slo_ref, shi_ref, q_ref):
    a = a_ref[0]
    cos, slo, shi = cos_ref[...], slo_ref[...], shi_ref[...]
    for h in range(MLA_HEADS):
        r = _dot(a, w_ref[h])
        q_ref[0, h, :, :LANES] = (r[:, :LANES] * Q_SCALE).astype(q_ref.dtype)
        pe = _rope(r[:, LANES:], cos, slo, shi)
        q_ref[0, h, :, LANES:] = (pe * Q_SCALE).astype(q_ref.dtype)


def _q_up(cqn, w, rope, tm):
    b, s, k = cqn.shape
    tab = pl.BlockSpec((tm, LANES), lambda bi, i: (i, 0))
    return pl.pallas_call(
        _q_up_kernel,
        grid=(b, s // tm),
        in_specs=[pl.BlockSpec((1, tm, k), lambda bi, i: (bi, i, 0)),
                  pl.BlockSpec((MLA_HEADS, k, QK_PAD), lambda bi, i: (0, 0, 0)), tab, tab, tab],
        out_specs=pl.BlockSpec((1, MLA_HEADS, tm, QK_PAD), lambda bi, i: (bi, 0, i, 0)),
        out_shape=jax.ShapeDtypeStruct((b, MLA_HEADS, s, QK_PAD), BF16),
        compiler_params=_params("arbitrary", "arbitrary"),
        name="q_up",
    )(cqn, w, *rope)


def _kv_up_kernel(a_ref, pe_ref, w_ref, k_ref, v_ref):
    a = a_ref[0]
    pe = pe_ref[0]
    width = QK_NOPE_DIM + V_HEAD_DIM
    ones = jnp.ones((a.shape[0], V_PAD - V_HEAD_DIM), v_ref.dtype)
    for h in range(MLA_HEADS):
        r = _dot(a, w_ref[:, h * width:(h + 1) * width])
        k_ref[0, h, :, :LANES] = r[:, :QK_NOPE_DIM].astype(k_ref.dtype)
        k_ref[0, h, :, LANES:] = pe
        v_ref[0, h, :, :V_HEAD_DIM] = r[:, QK_NOPE_DIM:].astype(v_ref.dtype)
        v_ref[0, h, :, V_HEAD_DIM:] = ones


def _kv_up(ckvn, kpe, w, tm):
    b, s, k = ckvn.shape
    return pl.pallas_call(
        _kv_up_kernel,
        grid=(b, s // tm),
        in_specs=[pl.BlockSpec((1, tm, k), lambda bi, i: (bi, i, 0)),
                  pl.BlockSpec((1, tm, LANES), lambda bi, i: (bi, i, 0)),
                  pl.BlockSpec(w.shape, lambda bi, i: (0, 0))],
        out_specs=[pl.BlockSpec((1, MLA_HEADS, tm, QK_PAD), lambda bi, i: (bi, 0, i, 0)),
                   pl.BlockSpec((1, MLA_HEADS, tm, V_PAD), lambda bi, i: (bi, 0, i, 0))],
        out_shape=[jax.ShapeDtypeStruct((b, MLA_HEADS, s, QK_PAD), BF16),
                   jax.ShapeDtypeStruct((b, MLA_HEADS, s, V_PAD), BF16)],
        compiler_params=_params("arbitrary", "arbitrary"),
        name="kv_up",
    )(ckvn, kpe, w)


def _flash_init(m_ref, acc_ref):
    m_ref[...] = jnp.full(m_ref.shape, -jnp.inf, F32)
    acc_ref[...] = jnp.zeros(acc_ref.shape, F32)


def _flash_update(q_ref, k, v, m_ref, acc_ref):
    tq = q_ref.shape[2]
    n_blk = k.shape[0] // LANES
    for g in range(tq // FLASH_GROUP_ROWS):
        rows = slice(g * FLASH_GROUP_ROWS, (g + 1) * FLASH_GROUP_ROWS)
        s = lax.dot_general(q_ref[0, 0, rows, :], k, (((1,), (1,)), ((), ())),
                            preferred_element_type=F32)
        blocks = [s[:, c * LANES:(c + 1) * LANES] for c in range(n_blk)]
        m_blk = functools.reduce(jnp.maximum, blocks)
        m_prev = m_ref[rows, :]
        m_new = jnp.maximum(m_prev, jnp.max(m_blk, axis=1, keepdims=True))
        p = jnp.concatenate([jnp.exp2(blk - m_new).astype(BF16) for blk in blocks], axis=1)
        alpha = jnp.exp2(m_prev - m_new)
        acc_ref[rows, :] = acc_ref[rows, :] * jnp.concatenate([alpha, alpha], axis=1) + _dot(p, v)
        m_ref[rows, :] = m_new


def _flash_finish(o_ref, acc_ref):
    acc = acc_ref[...]
    o_ref[0] = (acc[:, :V_HEAD_DIM] / acc[:, V_HEAD_DIM:]).astype(o_ref.dtype)


def _flash_kernel(q_ref, kc_ref, vc_ref, kl_ref, vl_ref, o_ref, m_ref, acc_ref, *, tk):
    _flash_init(m_ref, acc_ref)
    _flash_update(q_ref, kc_ref[0, 0], vc_ref[0, 0], m_ref, acc_ref)

    for j in range(kl_ref.shape[2] // tk):
        rows = slice(j * tk, (j + 1) * tk)
        _flash_update(q_ref, kl_ref[0, 0, rows, :], vl_ref[0, 0, rows, :], m_ref, acc_ref)
    _flash_finish(o_ref, acc_ref)


def _flash_ctx_kernel(q_ref, kc_ref, vc_ref, o_ref, m_ref, acc_ref):
    _flash_init(m_ref, acc_ref)
    _flash_update(q_ref, kc_ref[0, 0], vc_ref[0, 0], m_ref, acc_ref)
    _flash_finish(o_ref, acc_ref)


def _flash_scratch(tq):
    return [pltpu.VMEM((tq, LANES), F32), pltpu.VMEM((tq, V_PAD), F32)]


def _flash(q, kc, vc, kl, vl, tq, tk):
    b, nh, s, _ = q.shape
    n_ctx = kc.shape[2]
    n_lat = kl.shape[2]
    whole = lambda n, w: pl.BlockSpec((1, 1, n, w), lambda bi, h, i: (bi, h, 0, 0))
    return pl.pallas_call(
        functools.partial(_flash_kernel, tk=tk),
        grid=(b, nh, s // tq),
        in_specs=[pl.BlockSpec((1, 1, tq, QK_PAD), lambda bi, h, i: (bi, h, i, 0)),
                  whole(n_ctx, QK_PAD), whole(n_ctx, V_PAD),
                  whole(n_lat, QK_PAD), whole(n_lat, V_PAD)],
        out_specs=pl.BlockSpec((1, tq, V_HEAD_DIM), lambda bi, h, i: (bi, i, h)),
        out_shape=jax.ShapeDtypeStruct((b, s, nh * V_HEAD_DIM), BF16),
        scratch_shapes=_flash_scratch(tq),
        compiler_params=_params("arbitrary", "arbitrary", "arbitrary"),
        name="flash",
    )(q, kc, vc, kl, vl)


def _flash_ctx(q, kc, vc):
    b, nh, s, _ = q.shape
    blk = lambda w: pl.BlockSpec((1, 1, s, w), lambda bi, h: (bi, h, 0, 0))
    return pl.pallas_call(
        _flash_ctx_kernel,
        grid=(b, nh),
        in_specs=[blk(QK_PAD), blk(QK_PAD), blk(V_PAD)],
        out_specs=pl.BlockSpec((1, s, V_HEAD_DIM), lambda bi, h: (bi, 0, h)),
        out_shape=jax.ShapeDtypeStruct((b, s, nh * V_HEAD_DIM), BF16),
        scratch_shapes=_flash_scratch(s),
        compiler_params=_params("arbitrary", "arbitrary"),
        name="flash_ctx",
    )(q, kc, vc)


def _conv_kernel(bg_ref, cg_ref, xin_ref, cgp_ref, xinp_ref, cgn_ref, xinn_ref, w_ref, o_ref):
    i = pl.program_id(1)
    tm = cg_ref.shape[1]
    u = cg_ref[0].astype(F32) * xin_ref[0].astype(F32)
    u_prev = cgp_ref[0, HALO - 1:HALO, :].astype(F32) * xinp_ref[0, HALO - 1:HALO, :].astype(F32)
    u_next = cgn_ref[0, 0:1, :].astype(F32) * xinn_ref[0, 0:1, :].astype(F32)
    u_prev = jnp.where(i > 0, u_prev, 0.0)
    u_next = jnp.where(i < pl.num_programs(1) - 1, u_next, 0.0)
    row = lax.broadcasted_iota(jnp.int32, u.shape, 0)
    below = jnp.where(row == 0, u_prev, pltpu.roll(u, 1, 0))
    above = jnp.where(row == tm - 1, u_next, pltpu.roll(u, tm - 1, 0))
    y = below * w_ref[0:1, :] + u * w_ref[1:2, :] + above * w_ref[2:3, :]
    o_ref[0] = (bg_ref[0].astype(F32) * y).astype(o_ref.dtype)


def _conv_branch(f_conv, conv_w, tm):
    b, s, _ = f_conv.shape
    c = CONV_DIM
    hb = tm // HALO
    n_hb = s // HALO
    main = lambda col: pl.BlockSpec((1, tm, c), lambda bi, i: (bi, i, col))
    prev = lambda col: pl.BlockSpec((1, HALO, c), lambda bi, i: (bi, jnp.maximum(i * hb - 1, 0), col))
    nxt = lambda col: pl.BlockSpec((1, HALO, c), lambda bi, i: (bi, jnp.minimum((i + 1) * hb, n_hb - 1), col))
    return pl.pallas_call(
        _conv_kernel,
        grid=(b, s // tm),
        in_specs=[main(0), main(1), main(2), prev(1), prev(2), nxt(1), nxt(2),
                  pl.BlockSpec(conv_w.shape, lambda bi, i: (0, 0))],
        out_specs=pl.BlockSpec((1, tm, c), lambda bi, i: (bi, i, 0)),
        out_shape=jax.ShapeDtypeStruct((b, s, c), BF16),
        compiler_params=_params("arbitrary", "arbitrary"),
        name="conv",
    )(f_conv, f_conv, f_conv, f_conv, f_conv, f_conv, f_conv, conv_w)


def _pool_kernel(u_ref, up_ref, un_ref, w_ref, sc_ref, o_ref, *, seq_len):
    i = pl.program_id(1)
    tm = u_ref.shape[1]
    rows = tm + 2 * HALO
    prev = jnp.where(i > 0, up_ref[0].astype(F32), 0.0)
    nxt = jnp.where(i < pl.num_programs(1) - 1, un_ref[0].astype(F32), 0.0)
    ext = jnp.concatenate([prev, u_ref[0].astype(F32), nxt], axis=0)
    t = i * tm + lax.broadcasted_iota(jnp.int32, (tm, 1), 0)
    g = POOL_GROUP_DIM
    for gi, win in enumerate(POOL_WINDOWS):
        e = ext[:, gi * g:(gi + 1) * g]
        acc = e + pltpu.roll(e, 1, 0)
        half = 1
        while 2 * half < win:
            acc = pltpu.roll(acc, half, 0) + pltpu.roll(acc, rows - half, 0)
            half *= 2
        lo = jnp.maximum(t - win // 2, 0)
        hi = jnp.minimum(t + (win - win // 2), seq_len)
        cnt = (hi - lo).astype(F32)
        p = acc[HALO:HALO + tm] / cnt - e[HALO:HALO + tm]
        y = _dot(p.astype(BF16), w_ref[gi]) * sc_ref[:, gi * g:(gi + 1) * g]
        o_ref[0, :, gi * g:(gi + 1) * g] = y.astype(o_ref.dtype)


def _pool_branch(f_pool, pool_w, pool_scale, tm):
    b, s, c = f_pool.shape
    hb = tm // HALO
    n_hb = s // HALO
    return pl.pallas_call(
        functools.partial(_pool_kernel, seq_len=s),
        grid=(b, s // tm),
        in_specs=[pl.BlockSpec((1, tm, c), lambda bi, i: (bi, i, 0)),
                  pl.BlockSpec((1, HALO, c), lambda bi, i: (bi, jnp.maximum(i * hb - 1, 0), 0)),
                  pl.BlockSpec((1, HALO, c), lambda bi, i: (bi, jnp.minimum((i + 1) * hb, n_hb - 1), 0)),
                  pl.BlockSpec(pool_w.shape, lambda bi, i: (0, 0, 0)),
                  pl.BlockSpec((1, c), lambda bi, i: (0, 0))],
        out_specs=pl.BlockSpec((1, tm, c), lambda bi, i: (bi, i, 0)),
        out_shape=jax.ShapeDtypeStruct((b, s, c), BF16),
        compiler_params=_params("arbitrary", "arbitrary"),
        name="pool",
    )(f_pool, f_pool, f_pool, pool_w, pool_scale.reshape(1, c))


def _merge_kernel(h_ref, cy_ref, ay_ref, py_ref, wg_ref, wc_ref, wa_ref, wp_ref, o_ref):
    h = h_ref[...]
    out = _sigmoid(_dot(h, wg_ref[0])) * _dot(cy_ref[...], wc_ref[...])
    out += _sigmoid(_dot(h, wg_ref[1])) * _dot(ay_ref[...], wa_ref[...])
    out += _sigmoid(_dot(h, wg_ref[2])) * _dot(py_ref[...], wp_ref[...])
    o_ref[...] = out.astype(o_ref.dtype)


def _merge(h, conv_y, attn_y, pool_y, w_gate, w_conv_out, w_mla_out, w_pool_out, tm, tn):
    m, d = h.shape
    n = w_conv_out.shape[1]
    row = lambda a: pl.BlockSpec((tm, a.shape[1]), lambda i, j: (i, 0))
    col = lambda w: pl.BlockSpec((w.shape[0], tn), lambda i, j: (0, j))
    return pl.pallas_call(
        _merge_kernel,
        grid=(m // tm, n // tn),
        in_specs=[row(h), row(conv_y), row(attn_y), row(pool_y),
                  pl.BlockSpec((N_BRANCHES, d, tn), lambda i, j: (0, 0, j)),
                  col(w_conv_out), col(w_mla_out), col(w_pool_out)],
        out_specs=pl.BlockSpec((tm, tn), lambda i, j: (i, j)),
        out_shape=jax.ShapeDtypeStruct((m, n), BF16),
        compiler_params=_params("arbitrary", "arbitrary"),
        name="merge",
    )(h, conv_y, attn_y, pool_y, w_gate, w_conv_out, w_mla_out, w_pool_out)


def _mm_resid_kernel(a_ref, w_ref, x_ref, g_ref, o_ref):
    o_ref[0] = x_ref[0] + g_ref[0] * _dot(a_ref[0], w_ref[...])


def _matmul_resid(a, w, x, g, tm, tn, name):
    b, s, k = a.shape
    n = w.shape[1]
    return pl.pallas_call(
        _mm_resid_kernel,
        grid=(b, s // tm, n // tn),
        in_specs=[pl.BlockSpec((1, tm, k), lambda bi, i, j: (bi, i, 0)),
                  pl.BlockSpec((k, tn), lambda bi, i, j: (0, j)),
                  pl.BlockSpec((1, tm, tn), lambda bi, i, j: (bi, i, j)),
                  pl.BlockSpec((1, 1, tn), lambda bi, i, j: (bi, 0, j))],
        out_specs=pl.BlockSpec((1, tm, tn), lambda bi, i, j: (bi, i, j)),
        out_shape=jax.ShapeDtypeStruct((b, s, n), F32),
        compiler_params=_params("arbitrary", "arbitrary", "arbitrary"),
        name=name,
    )(a, w, x, g)


def _ffn_up_kernel(a_ref, wg_ref, wu_ref, o_ref):
    a = a_ref[...]
    gate = _dot(a, wg_ref[...])
    o_ref[...] = (gate * _sigmoid(gate) * _dot(a, wu_ref[...])).astype(o_ref.dtype)


def _ffn_up(h, w_gate, w_up, tm, tn):
    m, k = h.shape
    n = w_gate.shape[1]
    wspec = pl.BlockSpec((k, tn), lambda i, j: (0, j))
    return pl.pallas_call(
        _ffn_up_kernel,
        grid=(m // tm, n // tn),
        in_specs=[pl.BlockSpec((tm, k), lambda i, j: (i, 0)), wspec, wspec],
        out_specs=pl.BlockSpec((tm, tn), lambda i, j: (i, j)),
        out_shape=jax.ShapeDtypeStruct((m, n), BF16),
        compiler_params=_params("arbitrary", "arbitrary"),
        name="ffn_up",
    )(h, w_gate, w_up)


def _rope_tables(n_tokens, rotate):
    lane = jnp.arange(LANES)
    valid = lane < QK_ROPE_DIM
    if not rotate:
        cos = jnp.broadcast_to(valid.astype(F32), (n_tokens, LANES))
        zero = jnp.zeros((n_tokens, LANES), F32)
        return cos, zero, zero
    axis_dim = QK_ROPE_DIM // 2
    t = jnp.arange(n_tokens, dtype=jnp.int32)
    pos = jnp.stack([t // GRID_W, t % GRID_W], axis=1).astype(F32)
    inv_freq = 1.0 / (ROPE_THETA ** (jnp.arange(0, axis_dim, 2, dtype=F32) / axis_dim))
    axis = jnp.clip(lane // axis_dim, 0, 1)
    upper = (lane % axis_dim) >= axis_dim // 2
    ang = pos[:, axis] * inv_freq[lane % (axis_dim // 2)][None, :]
    cos = jnp.where(valid, jnp.cos(ang), 0.0)
    sin = jnp.sin(ang)
    sin_lo = jnp.where(valid & ~upper, -sin, 0.0)
    sin_hi = jnp.where(valid & upper, sin, 0.0)
    return cos, sin_lo, sin_hi


def _layer_weights(l, w_in, w_uq, w_ukv, w_conv_out, w_mla_out, w_pool_out, w_o, pool_w,
                   w_ffn_gate, w_ffn_up, w_ffn_down):
    d = D_MODEL
    wi = w_in[l]
    w_kv = jnp.pad(wi[:, COL_KV:COL_POOL], ((0, 0), (0, LANES - QK_ROPE_DIM)))
    wq = w_uq[l].reshape(Q_LORA_RANK, MLA_HEADS, QK_NOPE_DIM + QK_ROPE_DIM)
    wq = jnp.pad(wq, ((0, 0), (0, 0), (0, QK_PAD - QK_NOPE_DIM - QK_ROPE_DIM))).transpose(1, 0, 2)
    return dict(
        conv=wi[:, :COL_Q].astype(BF16),
        cq=wi[:, COL_Q:COL_KV].astype(BF16),
        ckv=w_kv.astype(BF16),
        pool=wi[:, COL_POOL:COL_GATE].astype(BF16),
        gate=wi[:, COL_GATE:].reshape(d, N_BRANCHES, d).transpose(1, 0, 2).astype(BF16),
        uq=wq.astype(BF16),
        ukv=w_ukv[l].astype(BF16),
        conv_out=w_conv_out[l].astype(BF16),
        mla_out=w_mla_out[l].astype(BF16),
        pool_out=w_pool_out[l].astype(BF16),
        o=w_o[l].astype(BF16),
        pool_w=pool_w[l].astype(BF16),
        ffn_gate=w_ffn_gate[l].astype(BF16),
        ffn_up=w_ffn_up[l].astype(BF16),
        ffn_down=w_ffn_down[l].astype(BF16),
    )


def _kv_path(h2d, b, s, w, kv_norm_w, rope, tm):
    ckvn, kpe = _ckv_proj(h2d, w["ckv"], kv_norm_w, rope, tm)
    return _kv_up(ckvn.reshape(b, s, -1), kpe.reshape(b, s, -1), w["ukv"], tm)


def kernel(x, c, ctx, c_ctx, w_mod, b_mod, norm_mix_w, norm_ffn_w, w_in, conv_w, w_conv_out,
           q_norm_w, kv_norm_w, w_uq, w_ukv, w_mla_out, pool_w, pool_scale, w_pool_out, w_o,
           w_ffn_gate, w_ffn_up, w_ffn_down, final_norm_w):
    b, n_lat, d = x.shape
    n_ctx = ctx.shape[1]
    rope_lat = _rope_tables(n_lat, True)
    rope_ctx = _rope_tables(n_ctx, False)

    cvec = jnp.concatenate([c, c_ctx[None, :], jnp.zeros((8 - b - 1, d), F32)], axis=0)
    mod_all = _mod(cvec, w_mod, b_mod).reshape(DEPTH, 8, N_MOD, d)

    tm_lat = 512
    tm_ctx = n_ctx
    h_ctx = ctx
    for l in range(DEPTH):
        last = l == DEPTH - 1
        w = _layer_weights(l, w_in, w_uq, w_ukv, w_conv_out, w_mla_out, w_pool_out, w_o, pool_w,
                           w_ffn_gate, w_ffn_up, w_ffn_down)
        mod_lat = [mod_all[l, :b, i].reshape(b, 1, d) for i in range(N_MOD)]
        mod_ctx = [jnp.broadcast_to(mod_all[l, b, i].reshape(1, 1, d), (b, 1, d)) for i in range(N_MOD)]

        def mix(stream, mod, s, rope, tm, kv_extra):
            sh1, sc1, g1, sh2, sc2, g2 = mod
            tm_mm = 2 * tm
            tm_res = min(2 * tm, s)
            h = _norm_mod(stream, norm_mix_w[l], sh1, sc1, tm)
            h2d = h.reshape(b * s, d)
            k_own, v_own = _kv_path(h2d, b, s, w, kv_norm_w[l], rope, tm)
            cqn = _cq_proj(h2d, w["cq"], q_norm_w[l], tm)
            q = _q_up(cqn.reshape(b, s, -1), w["uq"], rope, tm)
            if kv_extra is None:
                attn = _flash_ctx(q, k_own, v_own)
            else:
                attn = _flash(q, kv_extra[0], kv_extra[1], k_own, v_own, 1024, 2048)
            f_conv = _matmul(h2d, w["conv"], tm_mm, 1024, "in_conv")
            conv_y = _conv_branch(f_conv.reshape(b, s, -1), conv_w[l], tm)
            f_pool = _matmul(h2d, w["pool"], tm_mm, 1024, "in_pool")
            pool_y = _pool_branch(f_pool.reshape(b, s, -1), w["pool_w"], pool_scale[l], tm)
            merged = _merge(h2d, conv_y.reshape(b * s, -1), attn.reshape(b * s, -1),
                            pool_y.reshape(b * s, -1), w["gate"], w["conv_out"], w["mla_out"],
                            w["pool_out"], tm, 256)
            x1 = _matmul_resid(merged.reshape(b, s, d), w["o"], stream, g1, tm_res, 512, "w_o")
            h_ffn = _norm_mod(x1, norm_ffn_w[l], sh2, sc2, tm)
            hid = _ffn_up(h_ffn.reshape(b * s, d), w["ffn_gate"], w["ffn_up"], tm_mm, 256)
            x2 = _matmul_resid(hid.reshape(b, s, -1), w["ffn_down"], x1, g2, tm, 512, "ffn_down")
            return x2, (k_own, v_own)

        if last:
            hc = _norm_mod(h_ctx, norm_mix_w[l], mod_ctx[0], mod_ctx[1], tm_ctx)
            kv_c = _kv_path(hc.reshape(b * n_ctx, d), b, n_ctx, w, kv_norm_w[l], rope_ctx, tm_ctx)
        else:
            h_ctx, kv_c = mix(h_ctx, mod_ctx, n_ctx, rope_ctx, tm_ctx, None)
        x, _ = mix(x, mod_lat, n_lat, rope_lat, tm_lat, kv_c)

    return _final_norm(x.reshape(b * n_lat, d), final_norm_w, 512).reshape(b, n_lat, d)
```

```python
import functools

import jax
import jax.numpy as jnp
from jax import lax
from jax.experimental import pallas as pl
from jax.experimental.pallas import tpu as pltpu

F32 = jnp.float32
BF16 = jnp.bfloat16

D_MODEL = 4096
DEPTH = 2
GRID_W = 64
CONV_DIM = 1024
MLA_HEADS = 16
QK_NOPE_DIM = 128
QK_ROPE_DIM = 64
V_HEAD_DIM = 128
Q_LORA_RANK = 1024
KV_LORA_RANK = 512
ROPE_THETA = 10000.0
POOL_WINDOWS = (2, 4, 8, 16)
POOL_GROUP_DIM = 256
POOL_DIM = len(POOL_WINDOWS) * POOL_GROUP_DIM
N_BRANCHES = 3
N_MOD = 6
NORM_EPS = 1e-6

COL_Q = 3 * CONV_DIM
COL_KV = COL_Q + Q_LORA_RANK
COL_KPE = COL_KV + KV_LORA_RANK
COL_POOL = COL_KPE + QK_ROPE_DIM
COL_GATE = COL_POOL + POOL_DIM

LANES = 128
QK_PAD = 2 * LANES
HALO = 16
VMEM_LIMIT = 56 * 1024 * 1024
V_PAD = 2 * LANES
FLASH_GROUP_ROWS = 256
MOD_CHUNK = 256
FLASH_MAX_CHUNK = 3072
Q_SCALE =(QK_NOPE_DIM + QK_ROPE_DIM) ** -0.5 * 1.4426950408889634


def _params(*sem):
    return pltpu.CompilerParams(dimension_semantics=sem, vmem_limit_bytes=VMEM_LIMIT)


def _sigmoid(x):
    return 1.0 / (1.0 + jnp.exp(-x))


def _dot(a, b):
    return jnp.dot(a, b, preferred_element_type=F32)


def _rms(x, w):
    ms = jnp.mean(x * x, axis=-1, keepdims=True)
    return x * lax.rsqrt(ms + NORM_EPS) * w


def _rope(pe, cos, sin_lo, sin_hi):
    return (pe * cos + pltpu.roll(pe, LANES - 16, 1) * sin_lo
            + pltpu.roll(pe, 16, 1) * sin_hi)


def _mod_kernel(ct_ref, w_ref, b_ref, o_ref):
    @pl.when(pl.program_id(1) == 0)
    def _():
        o_ref[0] = jnp.broadcast_to(b_ref[0], o_ref.shape[1:])

    ct = ct_ref[...]
    at = ct * _sigmoid(ct)
    n_rows = o_ref.shape[1]
    cols = [jnp.broadcast_to(at[:, r:r + 1], (at.shape[0], MOD_CHUNK)) for r in range(n_rows)]

    def chunk(j, carry):
        sl = pl.ds(pl.multiple_of(j * MOD_CHUNK, MOD_CHUNK), MOD_CHUNK)
        w = w_ref[0, :, sl]
        rows = [jnp.sum(col * w, axis=0, keepdims=True) for col in cols]
        o_ref[0, :, sl] += jnp.concatenate(rows, axis=0)
        return carry

    lax.fori_loop(0, o_ref.shape[2] // MOD_CHUNK, chunk, 0)


def _mod(cvec, w_mod, b_mod):
    depth, d, n = w_mod.shape
    rows = cvec.shape[0]
    tk = LANES
    return pl.pallas_call(
        _mod_kernel,
        grid=(depth, d // tk),
        in_specs=[pl.BlockSpec((tk, rows), lambda l, k: (k, 0)),
                  pl.BlockSpec((1, tk, n), lambda l, k: (l, k, 0)),
                  pl.BlockSpec((1, 1, n), lambda l, k: (l, 0, 0))],
        out_specs=pl.BlockSpec((1, rows, n), lambda l, k: (l, 0, 0)),
        out_shape=jax.ShapeDtypeStruct((depth, rows, n), F32),
        compiler_params=_params("arbitrary", "arbitrary"),
        name="mod",
    )(cvec.T, w_mod, b_mod.reshape(depth, 1, n))


def _norm_mod_kernel(x_ref, w_ref, sh_ref, sc_ref, o_ref):
    y = _rms(x_ref[0], w_ref[...])
    o_ref[0] = (y * (1.0 + sc_ref[0]) + sh_ref[0]).astype(o_ref.dtype)


def _norm_mod(x, w, shift, scale, tm):
    b, s, d = x.shape
    vec = pl.BlockSpec((1, 1, d), lambda bi, i: (bi, 0, 0))
    return pl.pallas_call(
        _norm_mod_kernel,
        grid=(b, s // tm),
        in_specs=[pl.BlockSpec((1, tm, d), lambda bi, i: (bi, i, 0)),
                  pl.BlockSpec((1, d), lambda bi, i: (0, 0)), vec, vec],
        out_specs=pl.BlockSpec((1, tm, d), lambda bi, i: (bi, i, 0)),
        out_shape=jax.ShapeDtypeStruct((b, s, d), BF16),
        compiler_params=_params("arbitrary", "arbitrary"),
        name="norm_mod",
    )(x, w.reshape(1, d), shift, scale)


def _norm_kernel(x_ref, w_ref, o_ref):
    o_ref[...] = _rms(x_ref[...], w_ref[...])


def _final_norm(x, w, tm):
    m, d = x.shape
    return pl.pallas_call(
        _norm_kernel,
        grid=(m // tm,),
        in_specs=[pl.BlockSpec((tm, d), lambda i: (i, 0)),
                  pl.BlockSpec((1, d), lambda i: (0, 0))],
        out_specs=pl.BlockSpec((tm, d), lambda i: (i, 0)),
        out_shape=jax.ShapeDtypeStruct((m, d), F32),
        compiler_params=_params("arbitrary"),
        name="final_norm",
    )(x, w.reshape(1, d))


def _mm_kernel(a_ref, w_ref, o_ref):
    o_ref[...] = _dot(a_ref[...], w_ref[...]).astype(o_ref.dtype)


def _matmul(a, w, tm, tn, name):
    m, k = a.shape
    n = w.shape[1]
    return pl.pallas_call(
        _mm_kernel,
        grid=(m // tm, n // tn),
        in_specs=[pl.BlockSpec((tm, k), lambda i, j: (i, 0)),
                  pl.BlockSpec((k, tn), lambda i, j: (0, j))],
        out_specs=pl.BlockSpec((tm, tn), lambda i, j: (i, j)),
        out_shape=jax.ShapeDtypeStruct((m, n), BF16),
        compiler_params=_params("arbitrary", "arbitrary"),
        name=name,
    )(a, w)


def _cq_kernel(a_ref, w_ref, nw_ref, o_ref):
    o_ref[...] = _rms(_dot(a_ref[...], w_ref[...]), nw_ref[...]).astype(o_ref.dtype)


def _cq_proj(h, w, nw, tm):
    m, k = h.shape
    n = w.shape[1]
    return pl.pallas_call(
        _cq_kernel,
        grid=(m // tm,),
        in_specs=[pl.BlockSpec((tm, k), lambda i: (i, 0)),
                  pl.BlockSpec((k, n), lambda i: (0, 0)),
                  pl.BlockSpec((1, n), lambda i: (0, 0))],
        out_specs=pl.BlockSpec((tm, n), lambda i: (i, 0)),
        out_shape=jax.ShapeDtypeStruct((m, n), BF16),
        compiler_params=_params("arbitrary"),
        name="cq_proj",
    )(h, w, nw.reshape(1, n))


def _ckv_kernel(a_ref, w_ref, nw_ref, cos_ref, slo_ref, shi_ref, ckv_ref, kpe_ref):
    r = _dot(a_ref[...], w_ref[...])
    ckv_ref[...] = _rms(r[:, :KV_LORA_RANK], nw_ref[...]).astype(ckv_ref.dtype)
    pe = _rope(r[:, KV_LORA_RANK:], cos_ref[...], slo_ref[...], shi_ref[...])
    kpe_ref[...] = pe.astype(kpe_ref.dtype)


def _ckv_proj(h, w, nw, rope, tm):
    m, k = h.shape
    n = w.shape[1]
    n_pos = rope[0].shape[0] // tm
    tab = pl.BlockSpec((tm, LANES), lambda i: (i % n_pos, 0))
    return pl.pallas_call(
        _ckv_kernel,
        grid=(m // tm,),
        in_specs=[pl.BlockSpec((tm, k), lambda i: (i, 0)),
                  pl.BlockSpec((k, n), lambda i: (0, 0)),
                  pl.BlockSpec((1, KV_LORA_RANK), lambda i: (0, 0)), tab, tab, tab],
        out_specs=[pl.BlockSpec((tm, KV_LORA_RANK), lambda i: (i, 0)),
                   pl.BlockSpec((tm, LANES), lambda i: (i, 0))],
        out_shape=[jax.ShapeDtypeStruct((m, KV_LORA_RANK), BF16),
                   jax.ShapeDtypeStruct((m, LANES), BF16)],
        compiler_params=_params("arbitrary"),
        name="ckv_proj",
    )(h, w, nw.reshape(1, KV_LORA_RANK), *rope)


def _q_up_kernel(a_ref, w_ref, cos_ref, slo_ref, shi_ref, q_ref):
    a = a_ref[0]
    cos, slo, shi = cos_ref[...], slo_ref[...], shi_ref[...]
    for h in range(MLA_HEADS):
        r = _dot(a, w_ref[h])
        q_ref[0, h, :, :LANES] = (r[:, :LANES] * Q_SCALE).astype(q_ref.dtype)
        pe = _rope(r[:, LANES:], cos, slo, shi)
        q_ref[0, h, :, LANES:] = (pe * Q_SCALE).astype(q_ref.dtype)


def _q_up(cqn, w, rope, tm):
    b, s, k = cqn.shape
    tab = pl.BlockSpec((tm, LANES), lambda bi, i: (i, 0))
    return pl.pallas_call(
        _q_up_kernel,
        grid=(b, s // tm),
        in_specs=[pl.BlockSpec((1, tm, k), lambda bi, i: (bi, i, 0)),
                  pl.BlockSpec((MLA_HEADS, k, QK_PAD), lambda bi, i: (0, 0, 0)), tab, tab, tab],
        out_specs=pl.BlockSpec((1, MLA_HEADS, tm, QK_PAD), lambda bi, i: (bi, 0, i, 0)),
        out_shape=jax.ShapeDtypeStruct((b, MLA_HEADS, s, QK_PAD), BF16),
        compiler_params=_params("arbitrary", "arbitrary"),
        name="q_up",
    )(cqn, w, *rope)


def _kv_up_compute(a_ref, pe_ref, w_ref, k_ref, v_ref):
    a = a_ref[0]
    pe = pe_ref[0]
    width = QK_NOPE_DIM + V_HEAD_DIM
    ones = jnp.ones((a.shape[0], V_PAD - V_HEAD_DIM), v_ref.dtype)
    for h in range(MLA_HEADS):
        r = _dot(a, w_ref[:, h * width:(h + 1) * width])
        k_ref[0, h, :, :LANES] = r[:, :QK_NOPE_DIM].astype(k_ref.dtype)
        k_ref[0, h, :, LANES:] = pe
        v_ref[0, h, :, :V_HEAD_DIM] = r[:, QK_NOPE_DIM:].astype(v_ref.dtype)
        v_ref[0, h, :, V_HEAD_DIM:] = ones


def _kv_up_kernel(a_ref, pe_ref, w_ref, k_ref, v_ref):
    _kv_up_compute(a_ref, pe_ref, w_ref, k_ref, v_ref)


def _kv_up_prefix_kernel(a_ref, pe_ref, w_ref, kp_ref, vp_ref, k_ref, v_ref):
    @pl.when(pl.program_id(1) == 0)
    def _():
        k_ref[...] = kp_ref[...]
        v_ref[...] = vp_ref[...]

    @pl.when(pl.program_id(1) > 0)
    def _():
        _kv_up_compute(a_ref, pe_ref, w_ref, k_ref, v_ref)


def _kv_up(ckvn, kpe, w_all, layer, tm, prefix=None):
    b, s, k = ckvn.shape
    wspec = pl.BlockSpec((None,) + w_all.shape[1:], lambda bi, i: (layer, 0, 0))
    if prefix is None:
        n_pre, kernel, extra, extra_specs = 0, _kv_up_kernel, (), []
        row = lambda bi, i: (bi, i, 0)
    else:
        n_pre, kernel, extra = 1, _kv_up_prefix_kernel, tuple(prefix)
        assert prefix[0].shape[2] == tm
        row = lambda bi, i: (bi, jnp.maximum(i - 1, 0), 0)
        extra_specs = [pl.BlockSpec((1, MLA_HEADS, tm, QK_PAD), lambda bi, i: (bi, 0, 0, 0)),
                       pl.BlockSpec((1, MLA_HEADS, tm, V_PAD), lambda bi, i: (bi, 0, 0, 0))]
    s_out = s + n_pre * tm
    return pl.pallas_call(
        kernel,
        grid=(b, s_out // tm),
        in_specs=[pl.BlockSpec((1, tm, k), row), pl.BlockSpec((1, tm, LANES), row), wspec] + extra_specs,
        out_specs=[pl.BlockSpec((1, MLA_HEADS, tm, QK_PAD), lambda bi, i: (bi, 0, i, 0)),
                   pl.BlockSpec((1, MLA_HEADS, tm, V_PAD), lambda bi, i: (bi, 0, i, 0))],
        out_shape=[jax.ShapeDtypeStruct((b, MLA_HEADS, s_out, QK_PAD), BF16),
                   jax.ShapeDtypeStruct((b, MLA_HEADS, s_out, V_PAD), BF16)],
        compiler_params=_params("arbitrary", "arbitrary"),
        name="kv_up",
    )(ckvn, kpe, w_all, *extra)


def _flash_init(m_ref, acc_ref):
    m_ref[...] = jnp.full(m_ref.shape, -jnp.inf, F32)
    acc_ref[...] = jnp.zeros(acc_ref.shape, F32)


def _flash_update(q_ref, k, v, m_ref, acc_ref):
    tq = q_ref.shape[2]
    n_blk = k.shape[0] // LANES
    for g in range(tq // FLASH_GROUP_ROWS):
        rows = slice(g * FLASH_GROUP_ROWS, (g + 1) * FLASH_GROUP_ROWS)
        s = lax.dot_general(q_ref[0, 0, rows, :], k, (((1,), (1,)), ((), ())),
                            preferred_element_type=F32)
        blocks = [s[:, c * LANES:(c + 1) * LANES] for c in range(n_blk)]
        m_blk = functools.reduce(jnp.maximum, blocks)
        m_prev = m_ref[rows, :]
        m_new = jnp.maximum(m_prev, jnp.max(m_blk, axis=1, keepdims=True))
        p = jnp.concatenate([jnp.exp2(blk - m_new).astype(BF16) for blk in blocks], axis=1)
        alpha = jnp.exp2(m_prev - m_new)
        acc_ref[rows, :] = acc_ref[rows, :] * jnp.concatenate([alpha, alpha], axis=1) + _dot(p, v)
        m_ref[rows, :] = m_new


def _flash_finish(o_ref, acc_ref):
    acc = acc_ref[...]
    o_ref[0] = (acc[:, :V_HEAD_DIM] / acc[:, V_HEAD_DIM:]).astype(o_ref.dtype)


def _flash_kernel(q_ref, k_ref, v_ref, o_ref, m_ref, acc_ref, *, tk):
    _flash_init(m_ref, acc_ref)
    for j in range(k_ref.shape[2] // tk):
        rows = slice(j * tk, (j + 1) * tk)
        _flash_update(q_ref, k_ref[0, 0, rows, :], v_ref[0, 0, rows, :], m_ref, acc_ref)
    _flash_finish(o_ref, acc_ref)


def _kv_chunk(n_kv):
    mxu = 2 * LANES
    return max(t for t in range(mxu, FLASH_MAX_CHUNK + 1, mxu) if n_kv % t == 0)


def _flash(q, k, v, tq, tk):
    b, nh, s, _ = q.shape
    n_kv = k.shape[2]
    whole = lambda w: pl.BlockSpec((1, 1, n_kv, w), lambda bi, h, i: (bi, h, 0, 0))
    return pl.pallas_call(
        functools.partial(_flash_kernel, tk=tk),
        grid=(b, nh, s // tq),
        in_specs=[pl.BlockSpec((1, 1, tq, QK_PAD), lambda bi, h, i: (bi, h, i, 0)),
                  whole(QK_PAD), whole(V_PAD)],
        out_specs=pl.BlockSpec((1, tq, V_HEAD_DIM), lambda bi, h, i: (bi, i, h)),
        out_shape=jax.ShapeDtypeStruct((b, s, nh * V_HEAD_DIM), BF16),
        scratch_shapes=[pltpu.VMEM((tq, LANES), F32), pltpu.VMEM((tq, V_PAD), F32)],
        compiler_params=_params("arbitrary", "arbitrary", "arbitrary"),
        name="flash",
    )(q, k, v)


def _conv_kernel(bg_ref, cg_ref, xin_ref, cgp_ref, xinp_ref, cgn_ref, xinn_ref, w_ref, o_ref):
    i = pl.program_id(1)
    tm = cg_ref.shape[1]
    u = cg_ref[0].astype(F32) * xin_ref[0].astype(F32)
    u_prev = cgp_ref[0, HALO - 1:HALO, :].astype(F32) * xinp_ref[0, HALO - 1:HALO, :].astype(F32)
    u_next = cgn_ref[0, 0:1, :].astype(F32) * xinn_ref[0, 0:1, :].astype(F32)
    u_prev = jnp.where(i > 0, u_prev, 0.0)
    u_next = jnp.where(i < pl.num_programs(1) - 1, u_next, 0.0)
    row = lax.broadcasted_iota(jnp.int32, u.shape, 0)
    below = jnp.where(row == 0, u_prev, pltpu.roll(u, 1, 0))
    above = jnp.where(row == tm - 1, u_next, pltpu.roll(u, tm - 1, 0))
    y = below * w_ref[0:1, :] + u * w_ref[1:2, :] + above * w_ref[2:3, :]
    o_ref[0] = (bg_ref[0].astype(F32) * y).astype(o_ref.dtype)


def _conv_branch(f_conv, conv_w, tm):
    b, s, _ = f_conv.shape
    c = CONV_DIM
    hb = tm // HALO
    n_hb = s // HALO
    main = lambda col: pl.BlockSpec((1, tm, c), lambda bi, i: (bi, i, col))
    prev = lambda col: pl.BlockSpec((1, HALO, c), lambda bi, i: (bi, jnp.maximum(i * hb - 1, 0), col))
    nxt = lambda col: pl.BlockSpec((1, HALO, c), lambda bi, i: (bi, jnp.minimum((i + 1) * hb, n_hb - 1), col))
    return pl.pallas_call(
        _conv_kernel,
        grid=(b, s // tm),
        in_specs=[main(0), main(1), main(2), prev(1), prev(2), nxt(1), nxt(2),
                  pl.BlockSpec(conv_w.shape, lambda bi, i: (0, 0))],
        out_specs=pl.BlockSpec((1, tm, c), lambda bi, i: (bi, i, 0)),
        out_shape=jax.ShapeDtypeStruct((b, s, c), BF16),
        compiler_params=_params("arbitrary", "arbitrary"),
        name="conv",
    )(f_conv, f_conv, f_conv, f_conv, f_conv, f_conv, f_conv, conv_w)


def _pool_kernel(u_ref, up_ref, un_ref, w_ref, sc_ref, o_ref, *, seq_len):
    i = pl.program_id(1)
    tm = u_ref.shape[1]
    rows = tm + 2 * HALO
    prev = jnp.where(i > 0, up_ref[0].astype(F32), 0.0)
    nxt = jnp.where(i < pl.num_programs(1) - 1, un_ref[0].astype(F32), 0.0)
    ext = jnp.concatenate([prev, u_ref[0].astype(F32), nxt], axis=0)
    t = i * tm + lax.broadcasted_iota(jnp.int32, (tm, 1), 0)
    g = POOL_GROUP_DIM
    for gi, win in enumerate(POOL_WINDOWS):
        e = ext[:, gi * g:(gi + 1) * g]
        acc = e + pltpu.roll(e, 1, 0)
        half = 1
        while 2 * half < win:
            acc = pltpu.roll(acc, half, 0) + pltpu.roll(acc, rows - half, 0)
            half *= 2
        lo = jnp.maximum(t - win // 2, 0)
        hi = jnp.minimum(t + (win - win // 2), seq_len)
        cnt = (hi - lo).astype(F32)
        p = acc[HALO:HALO + tm] / cnt - e[HALO:HALO + tm]
        y = _dot(p.astype(BF16), w_ref[gi]) * sc_ref[:, gi * g:(gi + 1) * g]
        o_ref[0, :, gi * g:(gi + 1) * g] = y.astype(o_ref.dtype)


def _pool_branch(f_pool, pool_w_all, layer, pool_scale, tm):
    b, s, c = f_pool.shape
    hb = tm // HALO
    n_hb = s // HALO
    return pl.pallas_call(
        functools.partial(_pool_kernel, seq_len=s),
        grid=(b, s // tm),
        in_specs=[pl.BlockSpec((1, tm, c), lambda bi, i: (bi, i, 0)),
                  pl.BlockSpec((1, HALO, c), lambda bi, i: (bi, jnp.maximum(i * hb - 1, 0), 0)),
                  pl.BlockSpec((1, HALO, c), lambda bi, i: (bi, jnp.minimum((i + 1) * hb, n_hb - 1), 0)),
                  pl.BlockSpec((None,) + pool_w_all.shape[1:], lambda bi, i: (layer, 0, 0, 0)),
                  pl.BlockSpec((1, c), lambda bi, i: (0, 0))],
        out_specs=pl.BlockSpec((1, tm, c), lambda bi, i: (bi, i, 0)),
        out_shape=jax.ShapeDtypeStruct((b, s, c), BF16),
        compiler_params=_params("arbitrary", "arbitrary"),
        name="pool",
    )(f_pool, f_pool, f_pool, pool_w_all, pool_scale.reshape(1, c))


def _merge_kernel(h_ref, cy_ref, ay_ref, py_ref, wg0_ref, wg1_ref, wg2_ref, wc_ref, wa_ref, wp_ref,
                  o_ref):
    h = h_ref[...]
    out = _sigmoid(_dot(h, wg0_ref[...])) * _dot(cy_ref[...], wc_ref[...])
    out += _sigmoid(_dot(h, wg1_ref[...])) * _dot(ay_ref[...], wa_ref[...])
    out += _sigmoid(_dot(h, wg2_ref[...])) * _dot(py_ref[...], wp_ref[...])
    o_ref[...] = out.astype(o_ref.dtype)


def _merge(h, conv_y, attn_y, pool_y, w_gate, w_conv_out, w_mla_out, w_pool_out, layer, tm, tn):
    m, d = h.shape
    n = w_conv_out.shape[2]
    n_tiles = n // tn
    row = lambda a: pl.BlockSpec((tm, a.shape[1]), lambda i, j: (i, 0))
    col = lambda w: pl.BlockSpec((None, w.shape[1], tn), lambda i, j: (layer, 0, j))
    gate = lambda br: pl.BlockSpec((d, tn), lambda i, j: (0, br * n_tiles + j))
    return pl.pallas_call(
        _merge_kernel,
        grid=(m // tm, n_tiles),
        in_specs=[row(h), row(conv_y), row(attn_y), row(pool_y), gate(0), gate(1), gate(2),
                  col(w_conv_out), col(w_mla_out), col(w_pool_out)],
        out_specs=pl.BlockSpec((tm, tn), lambda i, j: (i, j)),
        out_shape=jax.ShapeDtypeStruct((m, n), BF16),
        compiler_params=_params("arbitrary", "arbitrary"),
        name="merge",
    )(h, conv_y, attn_y, pool_y, w_gate, w_gate, w_gate, w_conv_out, w_mla_out, w_pool_out)


def _mm_resid_kernel(a_ref, w_ref, x_ref, g_ref, o_ref):
    o_ref[0] = x_ref[0] + g_ref[0] * _dot(a_ref[0], w_ref[...])


def _matmul_resid(a, w_all, layer, x, g, tm, tn, name):
    b, s, k = a.shape
    n = w_all.shape[2]
    return pl.pallas_call(
        _mm_resid_kernel,
        grid=(b, s // tm, n // tn),
        in_specs=[pl.BlockSpec((1, tm, k), lambda bi, i, j: (bi, i, 0)),
                  pl.BlockSpec((None, k, tn), lambda bi, i, j: (layer, 0, j)),
                  pl.BlockSpec((1, tm, tn), lambda bi, i, j: (bi, i, j)),
                  pl.BlockSpec((1, 1, tn), lambda bi, i, j: (bi, 0, j))],
        out_specs=pl.BlockSpec((1, tm, tn), lambda bi, i, j: (bi, i, j)),
        out_shape=jax.ShapeDtypeStruct((b, s, n), F32),
        compiler_params=_params("arbitrary", "arbitrary", "arbitrary"),
        name=name,
    )(a, w_all, x, g)


def _ffn_up_kernel(a_ref, wg_ref, wu_ref, o_ref):
    a = a_ref[...]
    gate = _dot(a, wg_ref[...])
    o_ref[...] = (gate * _sigmoid(gate) * _dot(a, wu_ref[...])).astype(o_ref.dtype)


def _ffn_up(h, w_gate_all, w_up_all, layer, tm, tn):
    m, k = h.shape
    n = w_gate_all.shape[2]
    wspec = pl.BlockSpec((None, k, tn), lambda i, j: (layer, 0, j))
    return pl.pallas_call(
        _ffn_up_kernel,
        grid=(m // tm, n // tn),
        in_specs=[pl.BlockSpec((tm, k), lambda i, j: (i, 0)), wspec, wspec],
        out_specs=pl.BlockSpec((tm, tn), lambda i, j: (i, j)),
        out_shape=jax.ShapeDtypeStruct((m, n), BF16),
        compiler_params=_params("arbitrary", "arbitrary"),
        name="ffn_up",
    )(h, w_gate_all, w_up_all)


def _rope_tables(n_tokens, rotate):
    lane = jnp.arange(LANES)
    valid = lane < QK_ROPE_DIM
    if not rotate:
        cos = jnp.broadcast_to(valid.astype(F32), (n_tokens, LANES))
        zero = jnp.zeros((n_tokens, LANES), F32)
        return cos, zero, zero
    axis_dim = QK_ROPE_DIM // 2
    t = jnp.arange(n_tokens, dtype=jnp.int32)
    pos = jnp.stack([t // GRID_W, t % GRID_W], axis=1).astype(F32)
    inv_freq = 1.0 / (ROPE_THETA ** (jnp.arange(0, axis_dim, 2, dtype=F32) / axis_dim))
    axis = jnp.clip(lane // axis_dim, 0, 1)
    upper = (lane % axis_dim) >= axis_dim // 2
    ang = pos[:, axis] * inv_freq[lane % (axis_dim // 2)][None, :]
    cos = jnp.where(valid, jnp.cos(ang), 0.0)
    sin = jnp.sin(ang)
    sin_lo = jnp.where(valid & ~upper, -sin, 0.0)
    sin_hi = jnp.where(valid & upper, sin, 0.0)
    return cos, sin_lo, sin_hi


def _layer_weights(l, w_in, w_uq, w_ukv, w_conv_out, w_mla_out, w_pool_out, w_o, pool_w,
                   w_ffn_gate, w_ffn_up, w_ffn_down):
    d = D_MODEL
    wi = w_in[l]
    w_kv = jnp.pad(wi[:, COL_KV:COL_POOL], ((0, 0), (0, LANES - QK_ROPE_DIM)))
    wq = w_uq[l].reshape(Q_LORA_RANK, MLA_HEADS, QK_NOPE_DIM + QK_ROPE_DIM)
    wq = jnp.pad(wq, ((0, 0), (0, 0), (0, QK_PAD - QK_NOPE_DIM - QK_ROPE_DIM))).transpose(1, 0, 2)
    return dict(
        conv=wi[:, :COL_Q].astype(BF16),
        cq=wi[:, COL_Q:COL_KV].astype(BF16),
        ckv=w_kv.astype(BF16),
        pool=wi[:, COL_POOL:COL_GATE].astype(BF16),
        gate=wi[:, COL_GATE:].astype(BF16),
        uq=wq.astype(BF16),
        ukv=w_ukv.astype(BF16),
        conv_out=w_conv_out.astype(BF16),
        mla_out=w_mla_out.astype(BF16),
        pool_out=w_pool_out.astype(BF16),
        o=w_o.astype(BF16),
        pool_w=pool_w.astype(BF16),
        ffn_gate=w_ffn_gate.astype(BF16),
        ffn_up=w_ffn_up.astype(BF16),
        ffn_down=w_ffn_down.astype(BF16),
    )


def _kv_path(h2d, b, s, w, layer, kv_norm_w, rope, tm, prefix=None):
    ckvn, kpe = _ckv_proj(h2d, w["ckv"], kv_norm_w, rope, tm)
    tm_kv = tm if prefix is None else prefix[0].shape[2]
    return _kv_up(ckvn.reshape(b, s, -1), kpe.reshape(b, s, -1), w["ukv"], layer, tm_kv, prefix)


def kernel(x, c, ctx, c_ctx, w_mod, b_mod, norm_mix_w, norm_ffn_w, w_in, conv_w, w_conv_out,
           q_norm_w, kv_norm_w, w_uq, w_ukv, w_mla_out, pool_w, pool_scale, w_pool_out, w_o,
           w_ffn_gate, w_ffn_up, w_ffn_down, final_norm_w):
    b, n_lat, d = x.shape
    n_ctx = ctx.shape[1]
    rope_lat = _rope_tables(n_lat, True)
    rope_ctx = _rope_tables(n_ctx, False)

    cvec = jnp.concatenate([c, c_ctx[None, :]], axis=0)
    mod_all = _mod(cvec, w_mod, b_mod).reshape(DEPTH, b + 1, N_MOD, d)

    tm_lat = 512
    tm_ctx = n_ctx
    h_ctx = ctx
    for l in range(DEPTH):
        last = l == DEPTH - 1
        w = _layer_weights(l, w_in, w_uq, w_ukv, w_conv_out, w_mla_out, w_pool_out, w_o, pool_w,
                           w_ffn_gate, w_ffn_up, w_ffn_down)
        mod_lat = [mod_all[l, :b, i].reshape(b, 1, d) for i in range(N_MOD)]
        mod_ctx = [jnp.broadcast_to(mod_all[l, b, i].reshape(1, 1, d), (b, 1, d)) for i in range(N_MOD)]

        def mix(stream, mod, s, rope, tm, kv_prefix):
            sh1, sc1, g1, sh2, sc2, g2 = mod
            tm_mm = 2 * tm
            tm_res = min(2 * tm, s)
            h = _norm_mod(stream, norm_mix_w[l], sh1, sc1, tm)
            h2d = h.reshape(b * s, d)
            keys, values = _kv_path(h2d, b, s, w, l, kv_norm_w[l], rope, tm, kv_prefix)
            cqn = _cq_proj(h2d, w["cq"], q_norm_w[l], tm)
            q = _q_up(cqn.reshape(b, s, -1), w["uq"], rope, tm)
            n_kv = keys.shape[2]
            tq = min(s, 1024)
            attn = _flash(q, keys, values, tq, _kv_chunk(n_kv))
            f_conv = _matmul(h2d, w["conv"], tm_mm, 1024, "in_conv")
            conv_y = _conv_branch(f_conv.reshape(b, s, -1), conv_w[l], tm)
            f_pool = _matmul(h2d, w["pool"], tm_mm, 1024, "in_pool")
            pool_y = _pool_branch(f_pool.reshape(b, s, -1), w["pool_w"], l, pool_scale[l], tm)
            merged = _merge(h2d, conv_y.reshape(b * s, -1), attn.reshape(b * s, -1),
                            pool_y.reshape(b * s, -1), w["gate"], w["conv_out"], w["mla_out"],
                            w["pool_out"], l, tm, 256)
            x1 = _matmul_resid(merged.reshape(b, s, d), w["o"], l, stream, g1, tm_res, 512, "w_o")
            h_ffn = _norm_mod(x1, norm_ffn_w[l], sh2, sc2, tm)
            hid = _ffn_up(h_ffn.reshape(b * s, d), w["ffn_gate"], w["ffn_up"], l, tm_mm, 256)
            x2 = _matmul_resid(hid.reshape(b, s, -1), w["ffn_down"], l, x1, g2, tm, 512, "ffn_down")
            return x2, (keys, values)

        if last:
            hc = _norm_mod(h_ctx, norm_mix_w[l], mod_ctx[0], mod_ctx[1], tm_ctx)
            kv_c = _kv_path(hc.reshape(b * n_ctx, d), b, n_ctx, w, l, kv_norm_w[l], rope_ctx, tm_ctx)
        else:
            h_ctx, kv_c = mix(h_ctx, mod_ctx, n_ctx, rope_ctx, tm_ctx, None)
        x, _ = mix(x, mod_lat, n_lat, rope_lat, tm_lat, kv_c)

    return _final_norm(x.reshape(b * n_lat, d), final_norm_w, 512).reshape(b, n_lat, d)
```

```python
import functools

import jax
import jax.numpy as jnp
from jax import lax
from jax.experimental import pallas as pl
from jax.experimental.pallas import tpu as pltpu

F32 = jnp.float32
BF16 = jnp.bfloat16

D_MODEL = 4096
DEPTH = 2
GRID_W = 64
CONV_DIM = 1024
MLA_HEADS = 16
QK_NOPE_DIM = 128
QK_ROPE_DIM = 64
V_HEAD_DIM = 128
Q_LORA_RANK = 1024
KV_LORA_RANK = 512
ROPE_THETA = 10000.0
POOL_WINDOWS = (2, 4, 8, 16)
POOL_GROUP_DIM = 256
POOL_DIM = len(POOL_WINDOWS) * POOL_GROUP_DIM
N_BRANCHES = 3
N_MOD = 6
NORM_EPS = 1e-6

COL_Q = 3 * CONV_DIM
COL_KV = COL_Q + Q_LORA_RANK
COL_KPE = COL_KV + KV_LORA_RANK
COL_POOL = COL_KPE + QK_ROPE_DIM
COL_GATE = COL_POOL + POOL_DIM

LANES = 128
SUBLANES = 8
QK_PAD = 2 * LANES
HALO = 16
VMEM_LIMIT = 56 * 1024 * 1024
V_PAD = 2 * LANES
FLASH_GROUP_ROWS = 256
FLASH_MAX_CHUNK = 3072
Q_SCALE =(QK_NOPE_DIM + QK_ROPE_DIM) ** -0.5 * 1.4426950408889634


def _params(*sem):
    return pltpu.CompilerParams(dimension_semantics=sem, vmem_limit_bytes=VMEM_LIMIT)


def _sigmoid(x):
    return 1.0 / (1.0 + jnp.exp(-x))


def _dot(a, b):
    return jnp.dot(a, b, preferred_element_type=F32)


def _rms(x, w):
    ms = jnp.mean(x * x, axis=-1, keepdims=True)
    return x * lax.rsqrt(ms + NORM_EPS) * w


def _rope(pe, cos, sin_lo, sin_hi):
    return (pe * cos + pltpu.roll(pe, LANES - 16, 1) * sin_lo
            + pltpu.roll(pe, 16, 1) * sin_hi)


def _mod_kernel(c_ref, w_ref, b_ref, o_ref):
    c = c_ref[...]
    a = (c * _sigmoid(c)).astype(BF16)
    o_ref[0] = _dot(a, w_ref[0].astype(BF16)) + b_ref[0]


def _mod(cvec, w_mod, b_mod):
    depth, d, n = w_mod.shape
    rows = cvec.shape[0]
    tn = 1024
    return pl.pallas_call(
        _mod_kernel,
        grid=(depth, n // tn),
        in_specs=[pl.BlockSpec((rows, d), lambda l, j: (0, 0)),
                  pl.BlockSpec((1, d, tn), lambda l, j: (l, 0, j)),
                  pl.BlockSpec((1, 1, tn), lambda l, j: (l, 0, j))],
        out_specs=pl.BlockSpec((1, rows, tn), lambda l, j: (l, 0, j)),
        out_shape=jax.ShapeDtypeStruct((depth, rows, n), F32),
        compiler_params=_params("arbitrary", "arbitrary"),
        name="mod",
    )(cvec, w_mod, b_mod.reshape(depth, 1, n))


def _norm_mod_kernel(x_ref, w_ref, sh_ref, sc_ref, o_ref):
    y = _rms(x_ref[0], w_ref[...])
    o_ref[0] = (y * (1.0 + sc_ref[0]) + sh_ref[0]).astype(o_ref.dtype)


def _norm_mod(x, w, shift, scale, tm):
    b, s, d = x.shape
    vec = pl.BlockSpec((1, 1, d), lambda bi, i: (bi, 0, 0))
    return pl.pallas_call(
        _norm_mod_kernel,
        grid=(b, s // tm),
        in_specs=[pl.BlockSpec((1, tm, d), lambda bi, i: (bi, i, 0)),
                  pl.BlockSpec((1, d), lambda bi, i: (0, 0)), vec, vec],
        out_specs=pl.BlockSpec((1, tm, d), lambda bi, i: (bi, i, 0)),
        out_shape=jax.ShapeDtypeStruct((b, s, d), BF16),
        compiler_params=_params("arbitrary", "arbitrary"),
        name="norm_mod",
    )(x, w.reshape(1, d), shift, scale)


def _norm_kernel(x_ref, w_ref, o_ref):
    o_ref[...] = _rms(x_ref[...], w_ref[...])


def _final_norm(x, w, tm):
    m, d = x.shape
    return pl.pallas_call(
        _norm_kernel,
        grid=(m // tm,),
        in_specs=[pl.BlockSpec((tm, d), lambda i: (i, 0)),
                  pl.BlockSpec((1, d), lambda i: (0, 0))],
        out_specs=pl.BlockSpec((tm, d), lambda i: (i, 0)),
        out_shape=jax.ShapeDtypeStruct((m, d), F32),
        compiler_params=_params("arbitrary"),
        name="final_norm",
    )(x, w.reshape(1, d))


def _mm_kernel(a_ref, w_ref, o_ref):
    o_ref[...] = _dot(a_ref[...], w_ref[...]).astype(o_ref.dtype)


def _matmul(a, w, tm, tn, name):
    m, k = a.shape
    n = w.shape[1]
    return pl.pallas_call(
        _mm_kernel,
        grid=(m // tm, n // tn),
        in_specs=[pl.BlockSpec((tm, k), lambda i, j: (i, 0)),
                  pl.BlockSpec((k, tn), lambda i, j: (0, j))],
        out_specs=pl.BlockSpec((tm, tn), lambda i, j: (i, j)),
        out_shape=jax.ShapeDtypeStruct((m, n), BF16),
        compiler_params=_params("arbitrary", "arbitrary"),
        name=name,
    )(a, w)


def _cq_kernel(a_ref, w_ref, nw_ref, o_ref):
    o_ref[...] = _rms(_dot(a_ref[...], w_ref[...]), nw_ref[...]).astype(o_ref.dtype)


def _cq_proj(h, w, nw, tm):
    m, k = h.shape
    n = w.shape[1]
    return pl.pallas_call(
        _cq_kernel,
        grid=(m // tm,),
        in_specs=[pl.BlockSpec((tm, k), lambda i: (i, 0)),
                  pl.BlockSpec((k, n), lambda i: (0, 0)),
                  pl.BlockSpec((1, n), lambda i: (0, 0))],
        out_specs=pl.BlockSpec((tm, n), lambda i: (i, 0)),
        out_shape=jax.ShapeDtypeStruct((m, n), BF16),
        compiler_params=_params("arbitrary"),
        name="cq_proj",
    )(h, w, nw.reshape(1, n))


def _ckv_kernel(a_ref, w_ref, nw_ref, cos_ref, slo_ref, shi_ref, ckv_ref, kpe_ref):
    r = _dot(a_ref[...], w_ref[...])
    ckv_ref[...] = _rms(r[:, :KV_LORA_RANK], nw_ref[...]).astype(ckv_ref.dtype)
    pe = _rope(r[:, KV_LORA_RANK:], cos_ref[...], slo_ref[...], shi_ref[...])
    kpe_ref[...] = pe.astype(kpe_ref.dtype)


def _ckv_proj(h, w, nw, rope, tm):
    m, k = h.shape
    n = w.shape[1]
    n_pos = rope[0].shape[0] // tm
    tab = pl.BlockSpec((tm, LANES), lambda i: (i % n_pos, 0))
    return pl.pallas_call(
        _ckv_kernel,
        grid=(m // tm,),
        in_specs=[pl.BlockSpec((tm, k), lambda i: (i, 0)),
                  pl.BlockSpec((k, n), lambda i: (0, 0)),
                  pl.BlockSpec((1, KV_LORA_RANK), lambda i: (0, 0)), tab, tab, tab],
        out_specs=[pl.BlockSpec((tm, KV_LORA_RANK), lambda i: (i, 0)),
                   pl.BlockSpec((tm, LANES), lambda i: (i, 0))],
        out_shape=[jax.ShapeDtypeStruct((m, KV_LORA_RANK), BF16),
                   jax.ShapeDtypeStruct((m, LANES), BF16)],
        compiler_params=_params("arbitrary"),
        name="ckv_proj",
    )(h, w, nw.reshape(1, KV_LORA_RANK), *rope)


def _q_up_kernel(a_ref, w_ref, cos_ref, slo_ref, shi_ref, q_ref):
    a = a_ref[0]
    cos, slo, shi = cos_ref[...], slo_ref[...], shi_ref[...]
    for h in range(MLA_HEADS):
        r = _dot(a, w_ref[h])
        q_ref[0, h, :, :LANES] = (r[:, :LANES] * Q_SCALE).astype(q_ref.dtype)
        pe = _rope(r[:, LANES:], cos, slo, shi)
        q_ref[0, h, :, LANES:] = (pe * Q_SCALE).astype(q_ref.dtype)


def _q_up(cqn, w, rope, tm):
    b, s, k = cqn.shape
    tab = pl.BlockSpec((tm, LANES), lambda bi, i: (i, 0))
    return pl.pallas_call(
        _q_up_kernel,
        grid=(b, s // tm),
        in_specs=[pl.BlockSpec((1, tm, k), lambda bi, i: (bi, i, 0)),
                  pl.BlockSpec((MLA_HEADS, k, QK_PAD), lambda bi, i: (0, 0, 0)), tab, tab, tab],
        out_specs=pl.BlockSpec((1, MLA_HEADS, tm, QK_PAD), lambda bi, i: (bi, 0, i, 0)),
        out_shape=jax.ShapeDtypeStruct((b, MLA_HEADS, s, QK_PAD), BF16),
        compiler_params=_params("arbitrary", "arbitrary"),
        name="q_up",
    )(cqn, w, *rope)


def _kv_up_compute(a_ref, pe_ref, w_ref, k_ref, v_ref):
    a = a_ref[0]
    pe = pe_ref[0]
    width = QK_NOPE_DIM + V_HEAD_DIM
    ones = jnp.ones((a.shape[0], V_PAD - V_HEAD_DIM), v_ref.dtype)
    for h in range(MLA_HEADS):
        r = _dot(a, w_ref[:, h * width:(h + 1) * width])
        k_ref[0, h, :, :LANES] = r[:, :QK_NOPE_DIM].astype(k_ref.dtype)
        k_ref[0, h, :, LANES:] = pe
        v_ref[0, h, :, :V_HEAD_DIM] = r[:, QK_NOPE_DIM:].astype(v_ref.dtype)
        v_ref[0, h, :, V_HEAD_DIM:] = ones


def _kv_up_kernel(a_ref, pe_ref, w_ref, k_ref, v_ref):
    _kv_up_compute(a_ref, pe_ref, w_ref, k_ref, v_ref)


def _kv_up_prefix_kernel(a_ref, pe_ref, w_ref, kp_ref, vp_ref, k_ref, v_ref):
    @pl.when(pl.program_id(1) == 0)
    def _():
        k_ref[...] = kp_ref[...]
        v_ref[...] = vp_ref[...]

    @pl.when(pl.program_id(1) > 0)
    def _():
        _kv_up_compute(a_ref, pe_ref, w_ref, k_ref, v_ref)


def _kv_up(ckvn, kpe, w_all, layer, tm, prefix=None):
    b, s, k = ckvn.shape
    wspec = pl.BlockSpec((None,) + w_all.shape[1:], lambda bi, i: (layer, 0, 0))
    if prefix is None:
        n_pre, kernel, extra, extra_specs = 0, _kv_up_kernel, (), []
        row = lambda bi, i: (bi, i, 0)
    else:
        n_pre, kernel, extra = 1, _kv_up_prefix_kernel, tuple(prefix)
        assert prefix[0].shape[2] == tm
        row = lambda bi, i: (bi, jnp.maximum(i - 1, 0), 0)
        extra_specs = [pl.BlockSpec((1, MLA_HEADS, tm, QK_PAD), lambda bi, i: (bi, 0, 0, 0)),
                       pl.BlockSpec((1, MLA_HEADS, tm, V_PAD), lambda bi, i: (bi, 0, 0, 0))]
    s_out = s + n_pre * tm
    return pl.pallas_call(
        kernel,
        grid=(b, s_out // tm),
        in_specs=[pl.BlockSpec((1, tm, k), row), pl.BlockSpec((1, tm, LANES), row), wspec] + extra_specs,
        out_specs=[pl.BlockSpec((1, MLA_HEADS, tm, QK_PAD), lambda bi, i: (bi, 0, i, 0)),
                   pl.BlockSpec((1, MLA_HEADS, tm, V_PAD), lambda bi, i: (bi, 0, i, 0))],
        out_shape=[jax.ShapeDtypeStruct((b, MLA_HEADS, s_out, QK_PAD), BF16),
                   jax.ShapeDtypeStruct((b, MLA_HEADS, s_out, V_PAD), BF16)],
        compiler_params=_params("arbitrary", "arbitrary"),
        name="kv_up",
    )(ckvn, kpe, w_all, *extra)


def _flash_init(m_ref, acc_ref):
    m_ref[...] = jnp.full(m_ref.shape, -jnp.inf, F32)
    acc_ref[...] = jnp.zeros(acc_ref.shape, F32)


def _flash_update(q_ref, k, v, m_ref, acc_ref):
    tq = q_ref.shape[2]
    n_blk = k.shape[0] // LANES
    for g in range(tq // FLASH_GROUP_ROWS):
        rows = slice(g * FLASH_GROUP_ROWS, (g + 1) * FLASH_GROUP_ROWS)
        s = lax.dot_general(q_ref[0, 0, rows, :], k, (((1,), (1,)), ((), ())),
                            preferred_element_type=F32)
        blocks = [s[:, c * LANES:(c + 1) * LANES] for c in range(n_blk)]
        m_blk = functools.reduce(jnp.maximum, blocks)
        m_prev = m_ref[rows, :]
        m_new = jnp.maximum(m_prev, jnp.max(m_blk, axis=1, keepdims=True))
        p = jnp.concatenate([jnp.exp2(blk - m_new).astype(BF16) for blk in blocks], axis=1)
        alpha = jnp.exp2(m_prev - m_new)
        acc_ref[rows, :] = acc_ref[rows, :] * jnp.concatenate([alpha, alpha], axis=1) + _dot(p, v)
        m_ref[rows, :] = m_new


def _flash_finish(o_ref, acc_ref):
    acc = acc_ref[...]
    o_ref[0] = (acc[:, :V_HEAD_DIM] / acc[:, V_HEAD_DIM:]).astype(o_ref.dtype)


def _flash_kernel(q_ref, k_ref, v_ref, o_ref, m_ref, acc_ref, *, tk):
    _flash_init(m_ref, acc_ref)
    for j in range(k_ref.shape[2] // tk):
        rows = slice(j * tk, (j + 1) * tk)
        _flash_update(q_ref, k_ref[0, 0, rows, :], v_ref[0, 0, rows, :], m_ref, acc_ref)
    _flash_finish(o_ref, acc_ref)


def _kv_chunk(n_kv):
    mxu = 2 * LANES
    return max(t for t in range(mxu, FLASH_MAX_CHUNK + 1, mxu) if n_kv % t == 0)


def _flash(q, k, v, tq, tk):
    b, nh, s, _ = q.shape
    n_kv = k.shape[2]
    whole = lambda w: pl.BlockSpec((1, 1, n_kv, w), lambda bi, h, i: (bi, h, 0, 0))
    return pl.pallas_call(
        functools.partial(_flash_kernel, tk=tk),
        grid=(b, nh, s // tq),
        in_specs=[pl.BlockSpec((1, 1, tq, QK_PAD), lambda bi, h, i: (bi, h, i, 0)),
                  whole(QK_PAD), whole(V_PAD)],
        out_specs=pl.BlockSpec((1, tq, V_HEAD_DIM), lambda bi, h, i: (bi, i, h)),
        out_shape=jax.ShapeDtypeStruct((b, s, nh * V_HEAD_DIM), BF16),
        scratch_shapes=[pltpu.VMEM((tq, LANES), F32), pltpu.VMEM((tq, V_PAD), F32)],
        compiler_params=_params("arbitrary", "arbitrary", "arbitrary"),
        name="flash",
    )(q, k, v)


def _conv_kernel(bg_ref, cg_ref, xin_ref, cgp_ref, xinp_ref, cgn_ref, xinn_ref, w_ref, o_ref):
    i = pl.program_id(1)
    tm = cg_ref.shape[1]
    u = cg_ref[0].astype(F32) * xin_ref[0].astype(F32)
    u_prev = cgp_ref[0, HALO - 1:HALO, :].astype(F32) * xinp_ref[0, HALO - 1:HALO, :].astype(F32)
    u_next = cgn_ref[0, 0:1, :].astype(F32) * xinn_ref[0, 0:1, :].astype(F32)
    u_prev = jnp.where(i > 0, u_prev, 0.0)
    u_next = jnp.where(i < pl.num_programs(1) - 1, u_next, 0.0)
    row = lax.broadcasted_iota(jnp.int32, u.shape, 0)
    below = jnp.where(row == 0, u_prev, pltpu.roll(u, 1, 0))
    above = jnp.where(row == tm - 1, u_next, pltpu.roll(u, tm - 1, 0))
    y = below * w_ref[0:1, :] + u * w_ref[1:2, :] + above * w_ref[2:3, :]
    o_ref[0] = (bg_ref[0].astype(F32) * y).astype(o_ref.dtype)


def _conv_branch(f_conv, conv_w, tm):
    b, s, _ = f_conv.shape
    c = CONV_DIM
    hb = tm // HALO
    n_hb = s // HALO
    main = lambda col: pl.BlockSpec((1, tm, c), lambda bi, i: (bi, i, col))
    prev = lambda col: pl.BlockSpec((1, HALO, c), lambda bi, i: (bi, jnp.maximum(i * hb - 1, 0), col))
    nxt = lambda col: pl.BlockSpec((1, HALO, c), lambda bi, i: (bi, jnp.minimum((i + 1) * hb, n_hb - 1), col))
    return pl.pallas_call(
        _conv_kernel,
        grid=(b, s // tm),
        in_specs=[main(0), main(1), main(2), prev(1), prev(2), nxt(1), nxt(2),
                  pl.BlockSpec(conv_w.shape, lambda bi, i: (0, 0))],
        out_specs=pl.BlockSpec((1, tm, c), lambda bi, i: (bi, i, 0)),
        out_shape=jax.ShapeDtypeStruct((b, s, c), BF16),
        compiler_params=_params("arbitrary", "arbitrary"),
        name="conv",
    )(f_conv, f_conv, f_conv, f_conv, f_conv, f_conv, f_conv, conv_w)


def _pool_kernel(u_ref, up_ref, un_ref, w_ref, sc_ref, o_ref, *, seq_len):
    i = pl.program_id(1)
    tm = u_ref.shape[1]
    rows = tm + 2 * HALO
    prev = jnp.where(i > 0, up_ref[0].astype(F32), 0.0)
    nxt = jnp.where(i < pl.num_programs(1) - 1, un_ref[0].astype(F32), 0.0)
    ext = jnp.concatenate([prev, u_ref[0].astype(F32), nxt], axis=0)
    t = i * tm + lax.broadcasted_iota(jnp.int32, (tm, 1), 0)
    g = POOL_GROUP_DIM
    for gi, win in enumerate(POOL_WINDOWS):
        e = ext[:, gi * g:(gi + 1) * g]
        acc = e + pltpu.roll(e, 1, 0)
        half = 1
        while 2 * half < win:
            acc = pltpu.roll(acc, half, 0) + pltpu.roll(acc, rows - half, 0)
            half *= 2
        lo = jnp.maximum(t - win // 2, 0)
        hi = jnp.minimum(t + (win - win // 2), seq_len)
        cnt = (hi - lo).astype(F32)
        p = acc[HALO:HALO + tm] / cnt - e[HALO:HALO + tm]
        y = _dot(p.astype(BF16), w_ref[gi]) * sc_ref[:, gi * g:(gi + 1) * g]
        o_ref[0, :, gi * g:(gi + 1) * g] = y.astype(o_ref.dtype)


def _pool_branch(f_pool, pool_w_all, layer, pool_scale, tm):
    b, s, c = f_pool.shape
    hb = tm // HALO
    n_hb = s // HALO
    return pl.pallas_call(
        functools.partial(_pool_kernel, seq_len=s),
        grid=(b, s // tm),
        in_specs=[pl.BlockSpec((1, tm, c), lambda bi, i: (bi, i, 0)),
                  pl.BlockSpec((1, HALO, c), lambda bi, i: (bi, jnp.maximum(i * hb - 1, 0), 0)),
                  pl.BlockSpec((1, HALO, c), lambda bi, i: (bi, jnp.minimum((i + 1) * hb, n_hb - 1), 0)),
                  pl.BlockSpec((None,) + pool_w_all.shape[1:], lambda bi, i: (layer, 0, 0, 0)),
                  pl.BlockSpec((1, c), lambda bi, i: (0, 0))],
        out_specs=pl.BlockSpec((1, tm, c), lambda bi, i: (bi, i, 0)),
        out_shape=jax.ShapeDtypeStruct((b, s, c), BF16),
        compiler_params=_params("arbitrary", "arbitrary"),
        name="pool",
    )(f_pool, f_pool, f_pool, pool_w_all, pool_scale.reshape(1, c))


def _merge_kernel(h_ref, cy_ref, ay_ref, py_ref, wg0_ref, wg1_ref, wg2_ref, wc_ref, wa_ref, wp_ref,
                  o_ref):
    h = h_ref[...]
    out = _sigmoid(_dot(h, wg0_ref[...])) * _dot(cy_ref[...], wc_ref[...])
    out += _sigmoid(_dot(h, wg1_ref[...])) * _dot(ay_ref[...], wa_ref[...])
    out += _sigmoid(_dot(h, wg2_ref[...])) * _dot(py_ref[...], wp_ref[...])
    o_ref[...] = out.astype(o_ref.dtype)


def _merge(h, conv_y, attn_y, pool_y, w_gate, w_conv_out, w_mla_out, w_pool_out, layer, tm, tn):
    m, d = h.shape
    n = w_conv_out.shape[2]
    n_tiles = n // tn
    row = lambda a: pl.BlockSpec((tm, a.shape[1]), lambda i, j: (i, 0))
    col = lambda w: pl.BlockSpec((None, w.shape[1], tn), lambda i, j: (layer, 0, j))
    gate = lambda br: pl.BlockSpec((d, tn), lambda i, j: (0, br * n_tiles + j))
    return pl.pallas_call(
        _merge_kernel,
        grid=(m // tm, n_tiles),
        in_specs=[row(h), row(conv_y), row(attn_y), row(pool_y), gate(0), gate(1), gate(2),
                  col(w_conv_out), col(w_mla_out), col(w_pool_out)],
        out_specs=pl.BlockSpec((tm, tn), lambda i, j: (i, j)),
        out_shape=jax.ShapeDtypeStruct((m, n), BF16),
        compiler_params=_params("arbitrary", "arbitrary"),
        name="merge",
    )(h, conv_y, attn_y, pool_y, w_gate, w_gate, w_gate, w_conv_out, w_mla_out, w_pool_out)


def _mm_resid_kernel(a_ref, w_ref, x_ref, g_ref, o_ref):
    o_ref[0] = x_ref[0] + g_ref[0] * _dot(a_ref[0], w_ref[...])


def _matmul_resid(a, w_all, layer, x, g, tm, tn, name):
    b, s, k = a.shape
    n = w_all.shape[2]
    return pl.pallas_call(
        _mm_resid_kernel,
        grid=(b, s // tm, n // tn),
        in_specs=[pl.BlockSpec((1, tm, k), lambda bi, i, j: (bi, i, 0)),
                  pl.BlockSpec((None, k, tn), lambda bi, i, j: (layer, 0, j)),
                  pl.BlockSpec((1, tm, tn), lambda bi, i, j: (bi, i, j)),
                  pl.BlockSpec((1, 1, tn), lambda bi, i, j: (bi, 0, j))],
        out_specs=pl.BlockSpec((1, tm, tn), lambda bi, i, j: (bi, i, j)),
        out_shape=jax.ShapeDtypeStruct((b, s, n), F32),
        compiler_params=_params("arbitrary", "arbitrary", "arbitrary"),
        name=name,
    )(a, w_all, x, g)


def _ffn_up_kernel(a_ref, wg_ref, wu_ref, o_ref):
    a = a_ref[...]
    gate = _dot(a, wg_ref[...])
    o_ref[...] = (gate * _sigmoid(gate) * _dot(a, wu_ref[...])).astype(o_ref.dtype)


def _ffn_up(h, w_gate_all, w_up_all, layer, tm, tn):
    m, k = h.shape
    n = w_gate_all.shape[2]
    wspec = pl.BlockSpec((None, k, tn), lambda i, j: (layer, 0, j))
    return pl.pallas_call(
        _ffn_up_kernel,
        grid=(m // tm, n // tn),
        in_specs=[pl.BlockSpec((tm, k), lambda i, j: (i, 0)), wspec, wspec],
        out_specs=pl.BlockSpec((tm, tn), lambda i, j: (i, j)),
        out_shape=jax.ShapeDtypeStruct((m, n), BF16),
        compiler_params=_params("arbitrary", "arbitrary"),
        name="ffn_up",
    )(h, w_gate_all, w_up_all)


def _rope_tables(n_tokens, rotate):
    lane = jnp.arange(LANES)
    valid = lane < QK_ROPE_DIM
    if not rotate:
        cos = jnp.broadcast_to(valid.astype(F32), (n_tokens, LANES))
        zero = jnp.zeros((n_tokens, LANES), F32)
        return cos, zero, zero
    axis_dim = QK_ROPE_DIM // 2
    t = jnp.arange(n_tokens, dtype=jnp.int32)
    pos = jnp.stack([t // GRID_W, t % GRID_W], axis=1).astype(F32)
    inv_freq = 1.0 / (ROPE_THETA ** (jnp.arange(0, axis_dim, 2, dtype=F32) / axis_dim))
    axis = jnp.clip(lane // axis_dim, 0, 1)
    upper = (lane % axis_dim) >= axis_dim // 2
    ang = pos[:, axis] * inv_freq[lane % (axis_dim // 2)][None, :]
    cos = jnp.where(valid, jnp.cos(ang), 0.0)
    sin = jnp.sin(ang)
    sin_lo = jnp.where(valid & ~upper, -sin, 0.0)
    sin_hi = jnp.where(valid & upper, sin, 0.0)
    return cos, sin_lo, sin_hi


def _layer_weights(l, w_in, w_uq, w_ukv, w_conv_out, w_mla_out, w_pool_out, w_o, pool_w,
                   w_ffn_gate, w_ffn_up, w_ffn_down):
    d = D_MODEL
    wi = w_in[l]
    w_kv = jnp.pad(wi[:, COL_KV:COL_POOL], ((0, 0), (0, LANES - QK_ROPE_DIM)))
    wq = w_uq[l].reshape(Q_LORA_RANK, MLA_HEADS, QK_NOPE_DIM + QK_ROPE_DIM)
    wq = jnp.pad(wq, ((0, 0), (0, 0), (0, QK_PAD - QK_NOPE_DIM - QK_ROPE_DIM))).transpose(1, 0, 2)
    return dict(
        conv=wi[:, :COL_Q].astype(BF16),
        cq=wi[:, COL_Q:COL_KV].astype(BF16),
        ckv=w_kv.astype(BF16),
        pool=wi[:, COL_POOL:COL_GATE].astype(BF16),
        gate=wi[:, COL_GATE:].astype(BF16),
        uq=wq.astype(BF16),
        ukv=w_ukv.astype(BF16),
        conv_out=w_conv_out.astype(BF16),
        mla_out=w_mla_out.astype(BF16),
        pool_out=w_pool_out.astype(BF16),
        o=w_o.astype(BF16),
        pool_w=pool_w.astype(BF16),
        ffn_gate=w_ffn_gate.astype(BF16),
        ffn_up=w_ffn_up.astype(BF16),
        ffn_down=w_ffn_down.astype(BF16),
    )


def _kv_path(h2d, b, s, w, layer, kv_norm_w, rope, tm, prefix=None):
    ckvn, kpe = _ckv_proj(h2d, w["ckv"], kv_norm_w, rope, tm)
    tm_kv = tm if prefix is None else prefix[0].shape[2]
    return _kv_up(ckvn.reshape(b, s, -1), kpe.reshape(b, s, -1), w["ukv"], layer, tm_kv, prefix)


def kernel(x, c, ctx, c_ctx, w_mod, b_mod, norm_mix_w, norm_ffn_w, w_in, conv_w, w_conv_out,
           q_norm_w, kv_norm_w, w_uq, w_ukv, w_mla_out, pool_w, pool_scale, w_pool_out, w_o,
           w_ffn_gate, w_ffn_up, w_ffn_down, final_norm_w):
    b, n_lat, d = x.shape
    n_ctx = ctx.shape[1]
    rope_lat = _rope_tables(n_lat, True)
    rope_ctx = _rope_tables(n_ctx, False)

    cvec = jnp.concatenate([c, c_ctx[None, :], jnp.zeros((SUBLANES - b - 1, d), F32)], axis=0)
    mod_all = _mod(cvec, w_mod, b_mod).reshape(DEPTH, SUBLANES, N_MOD, d)

    tm_lat = 512
    tm_ctx = n_ctx
    h_ctx = ctx
    for l in range(DEPTH):
        last = l == DEPTH - 1
        w = _layer_weights(l, w_in, w_uq, w_ukv, w_conv_out, w_mla_out, w_pool_out, w_o, pool_w,
                           w_ffn_gate, w_ffn_up, w_ffn_down)
        mod_lat = [mod_all[l, :b, i].reshape(b, 1, d) for i in range(N_MOD)]
        mod_ctx = [jnp.broadcast_to(mod_all[l, b, i].reshape(1, 1, d), (b, 1, d)) for i in range(N_MOD)]

        def mix(stream, mod, s, rope, tm, kv_prefix):
            sh1, sc1, g1, sh2, sc2, g2 = mod
            tm_mm = 2 * tm
            tm_res = min(2 * tm, s)
            h = _norm_mod(stream, norm_mix_w[l], sh1, sc1, tm)
            h2d = h.reshape(b * s, d)
            keys, values = _kv_path(h2d, b, s, w, l, kv_norm_w[l], rope, tm, kv_prefix)
            cqn = _cq_proj(h2d, w["cq"], q_norm_w[l], tm)
            q = _q_up(cqn.reshape(b, s, -1), w["uq"], rope, tm)
            n_kv = keys.shape[2]
            tq = min(s, 2048)
            attn = _flash(q, keys, values, tq, _kv_chunk(n_kv))
            f_conv = _matmul(h2d, w["conv"], tm_mm, 1024, "in_conv")
            conv_y = _conv_branch(f_conv.reshape(b, s, -1), conv_w[l], tm)
            f_pool = _matmul(h2d, w["pool"], tm_mm, 1024, "in_pool")
            pool_y = _pool_branch(f_pool.reshape(b, s, -1), w["pool_w"], l, pool_scale[l], tm)
            merged = _merge(h2d, conv_y.reshape(b * s, -1), attn.reshape(b * s, -1),
                            pool_y.reshape(b * s, -1), w["gate"], w["conv_out"], w["mla_out"],
                            w["pool_out"], l, tm, 256)
            x1 = _matmul_resid(merged.reshape(b, s, d), w["o"], l, stream, g1, tm_res, 512, "w_o")
            h_ffn = _norm_mod(x1, norm_ffn_w[l], sh2, sc2, tm)
            hid = _ffn_up(h_ffn.reshape(b * s, d), w["ffn_gate"], w["ffn_up"], l,
                          min(4 * tm, b * s), 256)
            x2 = _matmul_resid(hid.reshape(b, s, -1), w["ffn_down"], l, x1, g2, tm, 512, "ffn_down")
            return x2, (keys, values)

        if last:
            hc = _norm_mod(h_ctx, norm_mix_w[l], mod_ctx[0], mod_ctx[1], tm_ctx)
            kv_c = _kv_path(hc.reshape(b * n_ctx, d), b, n_ctx, w, l, kv_norm_w[l], rope_ctx, tm_ctx)
        else:
            h_ctx, kv_c = mix(h_ctx, mod_ctx, n_ctx, rope_ctx, tm_ctx, None)
        x, _ = mix(x, mod_lat, n_lat, rope_lat, tm_lat, kv_c)

    return _final_norm(x.reshape(b * n_lat, d), final_norm_w, 512).reshape(b, n_lat, d)
```

```python
import functools

import jax
import jax.numpy as jnp
from jax import lax
from jax.experimental import pallas as pl
from jax.experimental.pallas import tpu as pltpu

F32 = jnp.float32
BF16 = jnp.bfloat16

D_MODEL = 4096
DEPTH = 2
GRID_W = 64
CONV_DIM = 1024
MLA_HEADS = 16
QK_NOPE_DIM = 128
QK_ROPE_DIM = 64
V_HEAD_DIM = 128
Q_LORA_RANK = 1024
KV_LORA_RANK = 512
ROPE_THETA = 10000.0
POOL_WINDOWS = (2, 4, 8, 16)
POOL_GROUP_DIM = 256
POOL_DIM = len(POOL_WINDOWS) * POOL_GROUP_DIM
N_BRANCHES = 3
N_MOD = 6
NORM_EPS = 1e-6

COL_Q = 3 * CONV_DIM
COL_KV = COL_Q + Q_LORA_RANK
COL_KPE = COL_KV + KV_LORA_RANK
COL_POOL = COL_KPE + QK_ROPE_DIM
COL_GATE = COL_POOL + POOL_DIM

LANES = 128
SUBLANES = 8
QK_PAD = 2 * LANES
HALO = 16
VMEM_LIMIT = 56 * 1024 * 1024
V_PAD = 2 * LANES
FLASH_GROUP_ROWS = 256
FLASH_MAX_CHUNK = 3072
Q_SCALE =(QK_NOPE_DIM + QK_ROPE_DIM) ** -0.5 * 1.4426950408889634


def _params(*sem):
    return pltpu.CompilerParams(dimension_semantics=sem, vmem_limit_bytes=VMEM_LIMIT)


def _sigmoid(x):
    return 1.0 / (1.0 + jnp.exp(-x))


def _dot(a, b):
    return jnp.dot(a, b, preferred_element_type=F32)


def _rms(x, w):
    ms = jnp.mean(x * x, axis=-1, keepdims=True)
    return x * lax.rsqrt(ms + NORM_EPS) * w


def _rope(pe, cos, sin_lo, sin_hi):
    return (pe * cos + pltpu.roll(pe, LANES - 16, 1) * sin_lo
            + pltpu.roll(pe, 16, 1) * sin_hi)


def _mod_kernel(c_ref, b_ref, *refs):
    w_refs, o_ref = refs[:-1], refs[-1]
    c = c_ref[...]
    a = (c * _sigmoid(c)).astype(BF16)
    tn = w_refs[0].shape[2]
    for i, w_ref in enumerate(w_refs):
        cols = slice(i * tn, (i + 1) * tn)
        o_ref[0, :, cols] = _dot(a, w_ref[0].astype(BF16)) + b_ref[0, :, cols]


def _mod(cvec, w_mod, b_mod):
    depth, d, n = w_mod.shape
    rows = cvec.shape[0]
    tn, n_streams = 256, 4
    step = tn * n_streams
    wspec = lambda i: pl.BlockSpec((1, d, tn), lambda l, j: (l, 0, j * n_streams + i))
    return pl.pallas_call(
        _mod_kernel,
        grid=(depth, n // step),
        in_specs=[pl.BlockSpec((rows, d), lambda l, j: (0, 0)),
                  pl.BlockSpec((1, 1, step), lambda l, j: (l, 0, j))]
                 + [wspec(i) for i in range(n_streams)],
        out_specs=pl.BlockSpec((1, rows, step), lambda l, j: (l, 0, j)),
        out_shape=jax.ShapeDtypeStruct((depth, rows, n), F32),
        compiler_params=_params("arbitrary", "arbitrary"),
        name="mod",
    )(cvec, b_mod.reshape(depth, 1, n), *([w_mod] * n_streams))


def _norm_mod_kernel(x_ref, w_ref, sh_ref, sc_ref, o_ref):
    y = _rms(x_ref[0], w_ref[...])
    o_ref[0] = (y * (1.0 + sc_ref[0]) + sh_ref[0]).astype(o_ref.dtype)


def _norm_mod(x, w, shift, scale, tm):
    b, s, d = x.shape
    vec = pl.BlockSpec((1, 1, d), lambda bi, i: (bi, 0, 0))
    return pl.pallas_call(
        _norm_mod_kernel,
        grid=(b, s // tm),
        in_specs=[pl.BlockSpec((1, tm, d), lambda bi, i: (bi, i, 0)),
                  pl.BlockSpec((1, d), lambda bi, i: (0, 0)), vec, vec],
        out_specs=pl.BlockSpec((1, tm, d), lambda bi, i: (bi, i, 0)),
        out_shape=jax.ShapeDtypeStruct((b, s, d), BF16),
        compiler_params=_params("arbitrary", "arbitrary"),
        name="norm_mod",
    )(x, w.reshape(1, d), shift, scale)


def _norm_kernel(x_ref, w_ref, o_ref):
    o_ref[...] = _rms(x_ref[...], w_ref[...])


def _final_norm(x, w, tm):
    m, d = x.shape
    return pl.pallas_call(
        _norm_kernel,
        grid=(m // tm,),
        in_specs=[pl.BlockSpec((tm, d), lambda i: (i, 0)),
                  pl.BlockSpec((1, d), lambda i: (0, 0))],
        out_specs=pl.BlockSpec((tm, d), lambda i: (i, 0)),
        out_shape=jax.ShapeDtypeStruct((m, d), F32),
        compiler_params=_params("arbitrary"),
        name="final_norm",
    )(x, w.reshape(1, d))


def _mm_kernel(a_ref, w_ref, o_ref):
    o_ref[...] = _dot(a_ref[...], w_ref[...]).astype(o_ref.dtype)


def _matmul(a, w, tm, tn, name):
    m, k = a.shape
    n = w.shape[1]
    return pl.pallas_call(
        _mm_kernel,
        grid=(m // tm, n // tn),
        in_specs=[pl.BlockSpec((tm, k), lambda i, j: (i, 0)),
                  pl.BlockSpec((k, tn), lambda i, j: (0, j))],
        out_specs=pl.BlockSpec((tm, tn), lambda i, j: (i, j)),
        out_shape=jax.ShapeDtypeStruct((m, n), BF16),
        compiler_params=_params("arbitrary", "arbitrary"),
        name=name,
    )(a, w)


def _cq_kernel(a_ref, w_ref, nw_ref, o_ref):
    o_ref[...] = _rms(_dot(a_ref[...], w_ref[...]), nw_ref[...]).astype(o_ref.dtype)


def _cq_proj(h, w, nw, tm):
    m, k = h.shape
    n = w.shape[1]
    return pl.pallas_call(
        _cq_kernel,
        grid=(m // tm,),
        in_specs=[pl.BlockSpec((tm, k), lambda i: (i, 0)),
                  pl.BlockSpec((k, n), lambda i: (0, 0)),
                  pl.BlockSpec((1, n), lambda i: (0, 0))],
        out_specs=pl.BlockSpec((tm, n), lambda i: (i, 0)),
        out_shape=jax.ShapeDtypeStruct((m, n), BF16),
        compiler_params=_params("arbitrary"),
        name="cq_proj",
    )(h, w, nw.reshape(1, n))


def _ckv_kernel(a_ref, w_ref, nw_ref, cos_ref, slo_ref, shi_ref, ckv_ref, kpe_ref):
    r = _dot(a_ref[...], w_ref[...])
    ckv_ref[...] = _rms(r[:, :KV_LORA_RANK], nw_ref[...]).astype(ckv_ref.dtype)
    pe = _rope(r[:, KV_LORA_RANK:], cos_ref[...], slo_ref[...], shi_ref[...])
    kpe_ref[...] = pe.astype(kpe_ref.dtype)


def _ckv_proj(h, w, nw, rope, tm):
    m, k = h.shape
    n = w.shape[1]
    n_pos = rope[0].shape[0] // tm
    tab = pl.BlockSpec((tm, LANES), lambda i: (i % n_pos, 0))
    return pl.pallas_call(
        _ckv_kernel,
        grid=(m // tm,),
        in_specs=[pl.BlockSpec((tm, k), lambda i: (i, 0)),
                  pl.BlockSpec((k, n), lambda i: (0, 0)),
                  pl.BlockSpec((1, KV_LORA_RANK), lambda i: (0, 0)), tab, tab, tab],
        out_specs=[pl.BlockSpec((tm, KV_LORA_RANK), lambda i: (i, 0)),
                   pl.BlockSpec((tm, LANES), lambda i: (i, 0))],
        out_shape=[jax.ShapeDtypeStruct((m, KV_LORA_RANK), BF16),
                   jax.ShapeDtypeStruct((m, LANES), BF16)],
        compiler_params=_params("arbitrary"),
        name="ckv_proj",
    )(h, w, nw.reshape(1, KV_LORA_RANK), *rope)


def _q_up_kernel(a_ref, w_ref, cos_ref, slo_ref, shi_ref, q_ref):
    a = a_ref[0]
    cos, slo, shi = cos_ref[...], slo_ref[...], shi_ref[...]
    for h in range(MLA_HEADS):
        r = _dot(a, w_ref[h])
        q_ref[0, h, :, :LANES] = (r[:, :LANES] * Q_SCALE).astype(q_ref.dtype)
        pe = _rope(r[:, LANES:], cos, slo, shi)
        q_ref[0, h, :, LANES:] = (pe * Q_SCALE).astype(q_ref.dtype)


def _q_up(cqn, w, rope, tm):
    b, s, k = cqn.shape
    tab = pl.BlockSpec((tm, LANES), lambda bi, i: (i, 0))
    return pl.pallas_call(
        _q_up_kernel,
        grid=(b, s // tm),
        in_specs=[pl.BlockSpec((1, tm, k), lambda bi, i: (bi, i, 0)),
                  pl.BlockSpec((MLA_HEADS, k, QK_PAD), lambda bi, i: (0, 0, 0)), tab, tab, tab],
        out_specs=pl.BlockSpec((1, MLA_HEADS, tm, QK_PAD), lambda bi, i: (bi, 0, i, 0)),
        out_shape=jax.ShapeDtypeStruct((b, MLA_HEADS, s, QK_PAD), BF16),
        compiler_params=_params("arbitrary", "arbitrary"),
        name="q_up",
    )(cqn, w, *rope)


def _kv_up_compute(a_ref, pe_ref, w_ref, k_ref, v_ref):
    a = a_ref[0]
    pe = pe_ref[0]
    width = QK_NOPE_DIM + V_HEAD_DIM
    ones = jnp.ones((a.shape[0], V_PAD - V_HEAD_DIM), v_ref.dtype)
    for h in range(MLA_HEADS):
        r = _dot(a, w_ref[:, h * width:(h + 1) * width])
        k_ref[0, h, :, :LANES] = r[:, :QK_NOPE_DIM].astype(k_ref.dtype)
        k_ref[0, h, :, LANES:] = pe
        v_ref[0, h, :, :V_HEAD_DIM] = r[:, QK_NOPE_DIM:].astype(v_ref.dtype)
        v_ref[0, h, :, V_HEAD_DIM:] = ones


def _kv_up_kernel(a_ref, pe_ref, w_ref, k_ref, v_ref):
    _kv_up_compute(a_ref, pe_ref, w_ref, k_ref, v_ref)


def _kv_up_prefix_kernel(a_ref, pe_ref, w_ref, kp_ref, vp_ref, k_ref, v_ref):
    @pl.when(pl.program_id(1) == 0)
    def _():
        k_ref[...] = kp_ref[...]
        v_ref[...] = vp_ref[...]

    @pl.when(pl.program_id(1) > 0)
    def _():
        _kv_up_compute(a_ref, pe_ref, w_ref, k_ref, v_ref)


def _kv_up(ckvn, kpe, w_all, layer, tm, prefix=None):
    b, s, k = ckvn.shape
    wspec = pl.BlockSpec((None,) + w_all.shape[1:], lambda bi, i: (layer, 0, 0))
    if prefix is None:
        n_pre, kernel, extra, extra_specs = 0, _kv_up_kernel, (), []
        row = lambda bi, i: (bi, i, 0)
    else:
        n_pre, kernel, extra = 1, _kv_up_prefix_kernel, tuple(prefix)
        assert prefix[0].shape[2] == tm
        row = lambda bi, i: (bi, jnp.maximum(i - 1, 0), 0)
        extra_specs = [pl.BlockSpec((1, MLA_HEADS, tm, QK_PAD), lambda bi, i: (bi, 0, 0, 0)),
                       pl.BlockSpec((1, MLA_HEADS, tm, V_PAD), lambda bi, i: (bi, 0, 0, 0))]
    s_out = s + n_pre * tm
    return pl.pallas_call(
        kernel,
        grid=(b, s_out // tm),
        in_specs=[pl.BlockSpec((1, tm, k), row), pl.BlockSpec((1, tm, LANES), row), wspec] + extra_specs,
        out_specs=[pl.BlockSpec((1, MLA_HEADS, tm, QK_PAD), lambda bi, i: (bi, 0, i, 0)),
                   pl.BlockSpec((1, MLA_HEADS, tm, V_PAD), lambda bi, i: (bi, 0, i, 0))],
        out_shape=[jax.ShapeDtypeStruct((b, MLA_HEADS, s_out, QK_PAD), BF16),
                   jax.ShapeDtypeStruct((b, MLA_HEADS, s_out, V_PAD), BF16)],
        compiler_params=_params("arbitrary", "arbitrary"),
        name="kv_up",
    )(ckvn, kpe, w_all, *extra)


def _flash_init(m_ref, acc_ref):
    m_ref[...] = jnp.full(m_ref.shape, -jnp.inf, F32)
    acc_ref[...] = jnp.zeros(acc_ref.shape, F32)


def _flash_update(q_ref, k, v, m_ref, acc_ref):
    tq = q_ref.shape[2]
    n_blk = k.shape[0] // LANES
    for g in range(tq // FLASH_GROUP_ROWS):
        rows = slice(g * FLASH_GROUP_ROWS, (g + 1) * FLASH_GROUP_ROWS)
        s = lax.dot_general(q_ref[0, 0, rows, :], k, (((1,), (1,)), ((), ())),
                            preferred_element_type=F32)
        blocks = [s[:, c * LANES:(c + 1) * LANES] for c in range(n_blk)]
        m_blk = functools.reduce(jnp.maximum, blocks)
        m_prev = m_ref[rows, :]
        m_new = jnp.maximum(m_prev, jnp.max(m_blk, axis=1, keepdims=True))
        p = jnp.concatenate([jnp.exp2(blk - m_new).astype(BF16) for blk in blocks], axis=1)
        alpha = jnp.exp2(m_prev - m_new)
        acc_ref[rows, :] = acc_ref[rows, :] * jnp.concatenate([alpha, alpha], axis=1) + _dot(p, v)
        m_ref[rows, :] = m_new


def _flash_finish(o_ref, acc_ref):
    acc = acc_ref[...]
    o_ref[0] = (acc[:, :V_HEAD_DIM] / acc[:, V_HEAD_DIM:]).astype(o_ref.dtype)


def _flash_kernel(q_ref, k_ref, v_ref, o_ref, m_ref, acc_ref, *, tk):
    _flash_init(m_ref, acc_ref)
    for j in range(k_ref.shape[2] // tk):
        rows = slice(j * tk, (j + 1) * tk)
        _flash_update(q_ref, k_ref[0, 0, rows, :], v_ref[0, 0, rows, :], m_ref, acc_ref)
    _flash_finish(o_ref, acc_ref)


def _kv_chunk(n_kv):
    mxu = 2 * LANES
    return max(t for t in range(mxu, FLASH_MAX_CHUNK + 1, mxu) if n_kv % t == 0)


def _flash(q, k, v, tq, tk):
    b, nh, s, _ = q.shape
    n_kv = k.shape[2]
    whole = lambda w: pl.BlockSpec((1, 1, n_kv, w), lambda bi, h, i: (bi, h, 0, 0))
    return pl.pallas_call(
        functools.partial(_flash_kernel, tk=tk),
        grid=(b, nh, s // tq),
        in_specs=[pl.BlockSpec((1, 1, tq, QK_PAD), lambda bi, h, i: (bi, h, i, 0)),
                  whole(QK_PAD), whole(V_PAD)],
        out_specs=pl.BlockSpec((1, tq, V_HEAD_DIM), lambda bi, h, i: (bi, i, h)),
        out_shape=jax.ShapeDtypeStruct((b, s, nh * V_HEAD_DIM), BF16),
        scratch_shapes=[pltpu.VMEM((tq, LANES), F32), pltpu.VMEM((tq, V_PAD), F32)],
        compiler_params=_params("arbitrary", "arbitrary", "arbitrary"),
        name="flash",
    )(q, k, v)


def _conv_kernel(bg_ref, cg_ref, xin_ref, cgp_ref, xinp_ref, cgn_ref, xinn_ref, w_ref, o_ref):
    i = pl.program_id(1)
    tm = cg_ref.shape[1]
    u = cg_ref[0].astype(F32) * xin_ref[0].astype(F32)
    u_prev = cgp_ref[0, HALO - 1:HALO, :].astype(F32) * xinp_ref[0, HALO - 1:HALO, :].astype(F32)
    u_next = cgn_ref[0, 0:1, :].astype(F32) * xinn_ref[0, 0:1, :].astype(F32)
    u_prev = jnp.where(i > 0, u_prev, 0.0)
    u_next = jnp.where(i < pl.num_programs(1) - 1, u_next, 0.0)
    row = lax.broadcasted_iota(jnp.int32, u.shape, 0)
    below = jnp.where(row == 0, u_prev, pltpu.roll(u, 1, 0))
    above = jnp.where(row == tm - 1, u_next, pltpu.roll(u, tm - 1, 0))
    y = below * w_ref[0:1, :] + u * w_ref[1:2, :] + above * w_ref[2:3, :]
    o_ref[0] = (bg_ref[0].astype(F32) * y).astype(o_ref.dtype)


def _conv_branch(f_conv, conv_w, tm):
    b, s, _ = f_conv.shape
    c = CONV_DIM
    hb = tm // HALO
    n_hb = s // HALO
    main = lambda col: pl.BlockSpec((1, tm, c), lambda bi, i: (bi, i, col))
    prev = lambda col: pl.BlockSpec((1, HALO, c), lambda bi, i: (bi, jnp.maximum(i * hb - 1, 0), col))
    nxt = lambda col: pl.BlockSpec((1, HALO, c), lambda bi, i: (bi, jnp.minimum((i + 1) * hb, n_hb - 1), col))
    return pl.pallas_call(
        _conv_kernel,
        grid=(b, s // tm),
        in_specs=[main(0), main(1), main(2), prev(1), prev(2), nxt(1), nxt(2),
                  pl.BlockSpec(conv_w.shape, lambda bi, i: (0, 0))],
        out_specs=pl.BlockSpec((1, tm, c), lambda bi, i: (bi, i, 0)),
        out_shape=jax.ShapeDtypeStruct((b, s, c), BF16),
        compiler_params=_params("arbitrary", "arbitrary"),
        name="conv",
    )(f_conv, f_conv, f_conv, f_conv, f_conv, f_conv, f_conv, conv_w)


def _pool_kernel(u_ref, up_ref, un_ref, w_ref, sc_ref, o_ref, *, seq_len):
    i = pl.program_id(1)
    tm = u_ref.shape[1]
    rows = tm + 2 * HALO
    prev = jnp.where(i > 0, up_ref[0].astype(F32), 0.0)
    nxt = jnp.where(i < pl.num_programs(1) - 1, un_ref[0].astype(F32), 0.0)
    ext = jnp.concatenate([prev, u_ref[0].astype(F32), nxt], axis=0)
    t = i * tm + lax.broadcasted_iota(jnp.int32, (tm, 1), 0)
    g = POOL_GROUP_DIM
    for gi, win in enumerate(POOL_WINDOWS):
        e = ext[:, gi * g:(gi + 1) * g]
        acc = e + pltpu.roll(e, 1, 0)
        half = 1
        while 2 * half < win:
            acc = pltpu.roll(acc, half, 0) + pltpu.roll(acc, rows - half, 0)
            half *= 2
        lo = jnp.maximum(t - win // 2, 0)
        hi = jnp.minimum(t + (win - win // 2), seq_len)
        cnt = (hi - lo).astype(F32)
        p = acc[HALO:HALO + tm] / cnt - e[HALO:HALO + tm]
        y = _dot(p.astype(BF16), w_ref[gi]) * sc_ref[:, gi * g:(gi + 1) * g]
        o_ref[0, :, gi * g:(gi + 1) * g] = y.astype(o_ref.dtype)


def _pool_branch(f_pool, pool_w_all, layer, pool_scale, tm):
    b, s, c = f_pool.shape
    hb = tm // HALO
    n_hb = s // HALO
    return pl.pallas_call(
        functools.partial(_pool_kernel, seq_len=s),
        grid=(b, s // tm),
        in_specs=[pl.BlockSpec((1, tm, c), lambda bi, i: (bi, i, 0)),
                  pl.BlockSpec((1, HALO, c), lambda bi, i: (bi, jnp.maximum(i * hb - 1, 0), 0)),
                  pl.BlockSpec((1, HALO, c), lambda bi, i: (bi, jnp.minimum((i + 1) * hb, n_hb - 1), 0)),
                  pl.BlockSpec((None,) + pool_w_all.shape[1:], lambda bi, i: (layer, 0, 0, 0)),
                  pl.BlockSpec((1, c), lambda bi, i: (0, 0))],
        out_specs=pl.BlockSpec((1, tm, c), lambda bi, i: (bi, i, 0)),
        out_shape=jax.ShapeDtypeStruct((b, s, c), BF16),
        compiler_params=_params("arbitrary", "arbitrary"),
        name="pool",
    )(f_pool, f_pool, f_pool, pool_w_all, pool_scale.reshape(1, c))


def _merge_kernel(h_ref, cy_ref, ay_ref, py_ref, wg0_ref, wg1_ref, wg2_ref, wc_ref, wa_ref, wp_ref,
                  o_ref):
    h = h_ref[...]
    out = _sigmoid(_dot(h, wg0_ref[...])) * _dot(cy_ref[...], wc_ref[...])
    out += _sigmoid(_dot(h, wg1_ref[...])) * _dot(ay_ref[...], wa_ref[...])
    out += _sigmoid(_dot(h, wg2_ref[...])) * _dot(py_ref[...], wp_ref[...])
    o_ref[...] = out.astype(o_ref.dtype)


def _merge(h, conv_y, attn_y, pool_y, w_gate, w_conv_out, w_mla_out, w_pool_out, layer, tm, tn):
    m, d = h.shape
    n = w_conv_out.shape[2]
    n_tiles = n // tn
    row = lambda a: pl.BlockSpec((tm, a.shape[1]), lambda i, j: (i, 0))
    col = lambda w: pl.BlockSpec((None, w.shape[1], tn), lambda i, j: (layer, 0, j))
    gate = lambda br: pl.BlockSpec((d, tn), lambda i, j: (0, br * n_tiles + j))
    return pl.pallas_call(
        _merge_kernel,
        grid=(m // tm, n_tiles),
        in_specs=[row(h), row(conv_y), row(attn_y), row(pool_y), gate(0), gate(1), gate(2),
                  col(w_conv_out), col(w_mla_out), col(w_pool_out)],
        out_specs=pl.BlockSpec((tm, tn), lambda i, j: (i, j)),
        out_shape=jax.ShapeDtypeStruct((m, n), BF16),
        compiler_params=_params("arbitrary", "arbitrary"),
        name="merge",
    )(h, conv_y, attn_y, pool_y, w_gate, w_gate, w_gate, w_conv_out, w_mla_out, w_pool_out)


def _mm_resid_kernel(a_ref, w_ref, x_ref, g_ref, o_ref):
    o_ref[0] = x_ref[0] + g_ref[0] * _dot(a_ref[0], w_ref[...])


def _matmul_resid(a, w_all, layer, x, g, tm, tn, name):
    b, s, k = a.shape
    n = w_all.shape[2]
    return pl.pallas_call(
        _mm_resid_kernel,
        grid=(b, s // tm, n // tn),
        in_specs=[pl.BlockSpec((1, tm, k), lambda bi, i, j: (bi, i, 0)),
                  pl.BlockSpec((None, k, tn), lambda bi, i, j: (layer, 0, j)),
                  pl.BlockSpec((1, tm, tn), lambda bi, i, j: (bi, i, j)),
                  pl.BlockSpec((1, 1, tn), lambda bi, i, j: (bi, 0, j))],
        out_specs=pl.BlockSpec((1, tm, tn), lambda bi, i, j: (bi, i, j)),
        out_shape=jax.ShapeDtypeStruct((b, s, n), F32),
        compiler_params=_params("arbitrary", "arbitrary", "arbitrary"),
        name=name,
    )(a, w_all, x, g)


def _ffn_up_kernel(a_ref, wg_ref, wu_ref, o_ref):
    a = a_ref[...]
    gate = _dot(a, wg_ref[...])
    o_ref[...] = (gate * _sigmoid(gate) * _dot(a, wu_ref[...])).astype(o_ref.dtype)


def _ffn_up(h, w_gate_all, w_up_all, layer, tm, tn):
    m, k = h.shape
    n = w_gate_all.shape[2]
    wspec = pl.BlockSpec((None, k, tn), lambda i, j: (layer, 0, j))
    return pl.pallas_call(
        _ffn_up_kernel,
        grid=(m // tm, n // tn),
        in_specs=[pl.BlockSpec((tm, k), lambda i, j: (i, 0)), wspec, wspec],
        out_specs=pl.BlockSpec((tm, tn), lambda i, j: (i, j)),
        out_shape=jax.ShapeDtypeStruct((m, n), BF16),
        compiler_params=_params("arbitrary", "arbitrary"),
        name="ffn_up",
    )(h, w_gate_all, w_up_all)


def _rope_tables(n_tokens, rotate):
    lane = jnp.arange(LANES)
    valid = lane < QK_ROPE_DIM
    if not rotate:
        cos = jnp.broadcast_to(valid.astype(F32), (n_tokens, LANES))
        zero = jnp.zeros((n_tokens, LANES), F32)
        return cos, zero, zero
    axis_dim = QK_ROPE_DIM // 2
    t = jnp.arange(n_tokens, dtype=jnp.int32)
    pos = jnp.stack([t // GRID_W, t % GRID_W], axis=1).astype(F32)
    inv_freq = 1.0 / (ROPE_THETA ** (jnp.arange(0, axis_dim, 2, dtype=F32) / axis_dim))
    axis = jnp.clip(lane // axis_dim, 0, 1)
    upper = (lane % axis_dim) >= axis_dim // 2
    ang = pos[:, axis] * inv_freq[lane % (axis_dim // 2)][None, :]
    cos = jnp.where(valid, jnp.cos(ang), 0.0)
    sin = jnp.sin(ang)
    sin_lo = jnp.where(valid & ~upper, -sin, 0.0)
    sin_hi = jnp.where(valid & upper, sin, 0.0)
    return cos, sin_lo, sin_hi


def _layer_weights(l, w_in, w_uq, w_ukv, w_conv_out, w_mla_out, w_pool_out, w_o, pool_w,
                   w_ffn_gate, w_ffn_up, w_ffn_down):
    d = D_MODEL
    wi = w_in[l]
    w_kv = jnp.pad(wi[:, COL_KV:COL_POOL], ((0, 0), (0, LANES - QK_ROPE_DIM)))
    wq = w_uq[l].reshape(Q_LORA_RANK, MLA_HEADS, QK_NOPE_DIM + QK_ROPE_DIM)
    wq = jnp.pad(wq, ((0, 0), (0, 0), (0, QK_PAD - QK_NOPE_DIM - QK_ROPE_DIM))).transpose(1, 0, 2)
    return dict(
        conv=wi[:, :COL_Q].astype(BF16),
        cq=wi[:, COL_Q:COL_KV].astype(BF16),
        ckv=w_kv.astype(BF16),
        pool=wi[:, COL_POOL:COL_GATE].astype(BF16),
        gate=wi[:, COL_GATE:].astype(BF16),
        uq=wq.astype(BF16),
        ukv=w_ukv.astype(BF16),
        conv_out=w_conv_out.astype(BF16),
        mla_out=w_mla_out.astype(BF16),
        pool_out=w_pool_out.astype(BF16),
        o=w_o.astype(BF16),
        pool_w=pool_w.astype(BF16),
        ffn_gate=w_ffn_gate.astype(BF16),
        ffn_up=w_ffn_up.astype(BF16),
        ffn_down=w_ffn_down.astype(BF16),
    )


def _kv_path(h2d, b, s, w, layer, kv_norm_w, rope, tm, prefix=None):
    ckvn, kpe = _ckv_proj(h2d, w["ckv"], kv_norm_w, rope, tm)
    tm_kv = tm if prefix is None else prefix[0].shape[2]
    return _kv_up(ckvn.reshape(b, s, -1), kpe.reshape(b, s, -1), w["ukv"], layer, tm_kv, prefix)


def kernel(x, c, ctx, c_ctx, w_mod, b_mod, norm_mix_w, norm_ffn_w, w_in, conv_w, w_conv_out,
           q_norm_w, kv_norm_w, w_uq, w_ukv, w_mla_out, pool_w, pool_scale, w_pool_out, w_o,
           w_ffn_gate, w_ffn_up, w_ffn_down, final_norm_w):
    b, n_lat, d = x.shape
    n_ctx = ctx.shape[1]
    rope_lat = _rope_tables(n_lat, True)
    rope_ctx = _rope_tables(n_ctx, False)

    cvec = jnp.concatenate([c, c_ctx[None, :], jnp.zeros((SUBLANES - b - 1, d), F32)], axis=0)
    mod_all = _mod(cvec, w_mod, b_mod).reshape(DEPTH, SUBLANES, N_MOD, d)

    tm_lat = 512
    tm_ctx = n_ctx
    h_ctx = ctx
    for l in range(DEPTH):
        last = l == DEPTH - 1
        w = _layer_weights(l, w_in, w_uq, w_ukv, w_conv_out, w_mla_out, w_pool_out, w_o, pool_w,
                           w_ffn_gate, w_ffn_up, w_ffn_down)
        mod_lat = [mod_all[l, :b, i].reshape(b, 1, d) for i in range(N_MOD)]
        mod_ctx = [jnp.broadcast_to(mod_all[l, b, i].reshape(1, 1, d), (b, 1, d)) for i in range(N_MOD)]

        def mix(stream, mod, s, rope, tm, kv_prefix):
            sh1, sc1, g1, sh2, sc2, g2 = mod
            tm_mm = 2 * tm
            tm_res = min(2 * tm, s)
            h = _norm_mod(stream, norm_mix_w[l], sh1, sc1, tm)
            h2d = h.reshape(b * s, d)
            keys, values = _kv_path(h2d, b, s, w, l, kv_norm_w[l], rope, tm_res, kv_prefix)
            cqn = _cq_proj(h2d, w["cq"], q_norm_w[l], tm_res)
            q = _q_up(cqn.reshape(b, s, -1), w["uq"], rope, tm)
            n_kv = keys.shape[2]
            tq = min(s, 2048)
            attn = _flash(q, keys, values, tq, _kv_chunk(n_kv))
            f_conv = _matmul(h2d, w["conv"], tm_mm, 1024, "in_conv")
            conv_y = _conv_branch(f_conv.reshape(b, s, -1), conv_w[l], tm)
            f_pool = _matmul(h2d, w["pool"], tm_mm, 1024, "in_pool")
            pool_y = _pool_branch(f_pool.reshape(b, s, -1), w["pool_w"], l, pool_scale[l], tm)
            merged = _merge(h2d, conv_y.reshape(b * s, -1), attn.reshape(b * s, -1),
                            pool_y.reshape(b * s, -1), w["gate"], w["conv_out"], w["mla_out"],
                            w["pool_out"], l, tm, 512)
            x1 = _matmul_resid(merged.reshape(b, s, d), w["o"], l, stream, g1, tm_res, 512, "w_o")
            h_ffn = _norm_mod(x1, norm_ffn_w[l], sh2, sc2, tm)
            hid = _ffn_up(h_ffn.reshape(b * s, d), w["ffn_gate"], w["ffn_up"], l,
                          min(4 * tm, b * s), 256)
            x2 = _matmul_resid(hid.reshape(b, s, -1), w["ffn_down"], l, x1, g2, tm, 512, "ffn_down")
            return x2, (keys, values)

        if last:
            hc = _norm_mod(h_ctx, norm_mix_w[l], mod_ctx[0], mod_ctx[1], tm_ctx)
            kv_c = _kv_path(hc.reshape(b * n_ctx, d), b, n_ctx, w, l, kv_norm_w[l], rope_ctx, tm_ctx)
        else:
            h_ctx, kv_c = mix(h_ctx, mod_ctx, n_ctx, rope_ctx, tm_ctx, None)
        x, _ = mix(x, mod_lat, n_lat, rope_lat, tm_lat, kv_c)

    return _final_norm(x.reshape(b * n_lat, d), final_norm_w, 512).reshape(b, n_lat, d)
```

```python
import functools

import jax
import jax.numpy as jnp
from jax import lax
from jax.experimental import pallas as pl
from jax.experimental.pallas import tpu as pltpu

F32 = jnp.float32
BF16 = jnp.bfloat16

D_MODEL = 4096
DEPTH = 2
GRID_W = 64
CONV_DIM = 1024
MLA_HEADS = 16
QK_NOPE_DIM = 128
QK_ROPE_DIM = 64
V_HEAD_DIM = 128
Q_LORA_RANK = 1024
KV_LORA_RANK = 512
ROPE_THETA = 10000.0
POOL_WINDOWS = (2, 4, 8, 16)
POOL_GROUP_DIM = 256
POOL_DIM = len(POOL_WINDOWS) * POOL_GROUP_DIM
N_BRANCHES = 3
N_MOD = 6
NORM_EPS = 1e-6

COL_Q = 3 * CONV_DIM
COL_KV = COL_Q + Q_LORA_RANK
COL_KPE = COL_KV + KV_LORA_RANK
COL_POOL = COL_KPE + QK_ROPE_DIM
COL_GATE = COL_POOL + POOL_DIM

LANES = 128
SUBLANES = 8
QK_PAD = 2 * LANES
BF16_SUBLANES = 16
HALO = BF16_SUBLANES
VMEM_LIMIT = 56 * 1024 * 1024
V_PAD = 2 * LANES
FLASH_GROUP_ROWS = 256
FLASH_MAX_CHUNK = 3072
Q_SCALE = (QK_NOPE_DIM + QK_ROPE_DIM) ** -0.5 * 1.4426950408889634


def _params(*sem):
    return pltpu.CompilerParams(dimension_semantics=sem, vmem_limit_bytes=VMEM_LIMIT)


def _sigmoid(x):
    return 1.0 / (1.0 + jnp.exp(-x))


def _dot(a, b):
    return jnp.dot(a, b, preferred_element_type=F32)


def _rms(x, w):
    ms = jnp.mean(x * x, axis=-1, keepdims=True)
    return x * lax.rsqrt(ms + NORM_EPS) * w


def _rope(pe, cos, sin_lo, sin_hi):
    return (pe * cos + pltpu.roll(pe, LANES - 16, 1) * sin_lo
            + pltpu.roll(pe, 16, 1) * sin_hi)


def _cast_specs(srcs, layer, n_steps, step):
    ins, outs, shapes = [], [], []
    for src in srcs:
        _, rows, cols = src.shape
        assert rows % n_steps == 0 and (rows // n_steps) % BF16_SUBLANES == 0
        r = rows // n_steps
        ins.append(pl.BlockSpec((None, r, cols), lambda *g: (layer, step(*g), 0)))
        outs.append(pl.BlockSpec((r, cols), lambda *g: (step(*g), 0)))
        shapes.append(jax.ShapeDtypeStruct((rows, cols), BF16))
    return ins, outs, shapes


def _split_riders(rest):
    n = (len(rest) - 1) // 2
    return rest[:n], rest[n], rest[n + 1:]


def _run_casts(srcs, dsts):
    for src, dst in zip(srcs, dsts):
        dst[...] = src[...].astype(dst.dtype)


def _mod_kernel(c_ref, b_ref, *refs):
    w_refs, o_ref = refs[:-1], refs[-1]
    c = c_ref[...]
    a = (c * _sigmoid(c)).astype(BF16)
    tn = w_refs[0].shape[2]
    for i, w_ref in enumerate(w_refs):
        cols = slice(i * tn, (i + 1) * tn)
        o_ref[0, :, cols] = _dot(a, w_ref[0].astype(BF16)) + b_ref[0, :, cols]


def _mod(cvec, w_mod, b_mod):
    depth, d, n = w_mod.shape
    rows = cvec.shape[0]
    tn, n_streams = 256, 4
    step = tn * n_streams
    wspec = lambda i: pl.BlockSpec((1, d, tn), lambda l, j: (l, 0, j * n_streams + i))
    return pl.pallas_call(
        _mod_kernel,
        grid=(depth, n // step),
        in_specs=[pl.BlockSpec((rows, d), lambda l, j: (0, 0)),
                  pl.BlockSpec((1, 1, step), lambda l, j: (l, 0, j))]
                 + [wspec(i) for i in range(n_streams)],
        out_specs=pl.BlockSpec((1, rows, step), lambda l, j: (l, 0, j)),
        out_shape=jax.ShapeDtypeStruct((depth, rows, n), F32),
        compiler_params=_params("arbitrary", "arbitrary"),
        name="mod",
    )(cvec, b_mod.reshape(depth, 1, n), *([w_mod] * n_streams))


def _norm_mod_kernel(x_ref, w_ref, sh_ref, sc_ref, o_ref):
    y = _rms(x_ref[0], w_ref[...])
    o_ref[0] = (y * (1.0 + sc_ref[0]) + sh_ref[0]).astype(o_ref.dtype)


def _norm_mod(x, w, shift, scale, tm):
    b, s, d = x.shape
    vec = pl.BlockSpec((1, 1, d), lambda bi, i: (bi, 0, 0))
    return pl.pallas_call(
        _norm_mod_kernel,
        grid=(b, s // tm),
        in_specs=[pl.BlockSpec((1, tm, d), lambda bi, i: (bi, i, 0)),
                  pl.BlockSpec((1, d), lambda bi, i: (0, 0)), vec, vec],
        out_specs=pl.BlockSpec((1, tm, d), lambda bi, i: (bi, i, 0)),
        out_shape=jax.ShapeDtypeStruct((b, s, d), BF16),
        compiler_params=_params("arbitrary", "arbitrary"),
        name="norm_mod",
    )(x, w.reshape(1, d), shift, scale)


def _norm_kernel(x_ref, w_ref, o_ref):
    o_ref[...] = _rms(x_ref[...], w_ref[...])


def _final_norm(x, w, tm):
    m, d = x.shape
    return pl.pallas_call(
        _norm_kernel,
        grid=(m // tm,),
        in_specs=[pl.BlockSpec((tm, d), lambda i: (i, 0)),
                  pl.BlockSpec((1, d), lambda i: (0, 0))],
        out_specs=pl.BlockSpec((tm, d), lambda i: (i, 0)),
        out_shape=jax.ShapeDtypeStruct((m, d), F32),
        compiler_params=_params("arbitrary"),
        name="final_norm",
    )(x, w.reshape(1, d))


def _mm_kernel(a_ref, w_ref, o_ref):
    o_ref[...] = _dot(a_ref[...], w_ref[...]).astype(o_ref.dtype)


def _matmul(a, w, tm, tn, name):
    m, k = a.shape
    n = w.shape[1]
    return pl.pallas_call(
        _mm_kernel,
        grid=(m // tm, n // tn),
        in_specs=[pl.BlockSpec((tm, k), lambda i, j: (i, 0)),
                  pl.BlockSpec((k, tn), lambda i, j: (0, j))],
        out_specs=pl.BlockSpec((tm, tn), lambda i, j: (i, j)),
        out_shape=jax.ShapeDtypeStruct((m, n), BF16),
        compiler_params=_params("arbitrary", "arbitrary"),
        name=name,
    )(a, w)


def _cq_kernel(a_ref, w_ref, nw_ref, o_ref):
    o_ref[...] = _rms(_dot(a_ref[...], w_ref[...]), nw_ref[...]).astype(o_ref.dtype)


def _cq_proj(h, w, nw, tm):
    m, k = h.shape
    n = w.shape[1]
    return pl.pallas_call(
        _cq_kernel,
        grid=(m // tm,),
        in_specs=[pl.BlockSpec((tm, k), lambda i: (i, 0)),
                  pl.BlockSpec((k, n), lambda i: (0, 0)),
                  pl.BlockSpec((1, n), lambda i: (0, 0))],
        out_specs=pl.BlockSpec((tm, n), lambda i: (i, 0)),
        out_shape=jax.ShapeDtypeStruct((m, n), BF16),
        compiler_params=_params("arbitrary"),
        name="cq_proj",
    )(h, w, nw.reshape(1, n))


def _ckv_kernel(a_ref, w_ref, nw_ref, cos_ref, slo_ref, shi_ref, ckv_ref, kpe_ref):
    r = _dot(a_ref[...], w_ref[...])
    ckv_ref[...] = _rms(r[:, :KV_LORA_RANK], nw_ref[...]).astype(ckv_ref.dtype)
    pe = _rope(r[:, KV_LORA_RANK:], cos_ref[...], slo_ref[...], shi_ref[...])
    kpe_ref[...] = pe.astype(kpe_ref.dtype)


def _ckv_proj(h, w, nw, rope, tm):
    m, k = h.shape
    n = w.shape[1]
    n_pos = rope[0].shape[0] // tm
    tab = pl.BlockSpec((tm, LANES), lambda i: (i % n_pos, 0))
    return pl.pallas_call(
        _ckv_kernel,
        grid=(m // tm,),
        in_specs=[pl.BlockSpec((tm, k), lambda i: (i, 0)),
                  pl.BlockSpec((k, n), lambda i: (0, 0)),
                  pl.BlockSpec((1, KV_LORA_RANK), lambda i: (0, 0)), tab, tab, tab],
        out_specs=[pl.BlockSpec((tm, KV_LORA_RANK), lambda i: (i, 0)),
                   pl.BlockSpec((tm, LANES), lambda i: (i, 0))],
        out_shape=[jax.ShapeDtypeStruct((m, KV_LORA_RANK), BF16),
                   jax.ShapeDtypeStruct((m, LANES), BF16)],
        compiler_params=_params("arbitrary"),
        name="ckv_proj",
    )(h, w, nw.reshape(1, KV_LORA_RANK), *rope)


def _q_up_kernel(a_ref, w_ref, cos_ref, slo_ref, shi_ref, q_ref):
    a = a_ref[0]
    cos, slo, shi = cos_ref[...], slo_ref[...], shi_ref[...]
    for h in range(MLA_HEADS):
        r = _dot(a, w_ref[h])
        q_ref[0, h, :, :LANES] = (r[:, :LANES] * Q_SCALE).astype(q_ref.dtype)
        pe = _rope(r[:, LANES:], cos, slo, shi)
        q_ref[0, h, :, LANES:] = (pe * Q_SCALE).astype(q_ref.dtype)


def _q_up(cqn, w, rope, tm):
    b, s, k = cqn.shape
    tab = pl.BlockSpec((tm, LANES), lambda bi, i: (i, 0))
    return pl.pallas_call(
        _q_up_kernel,
        grid=(b, s // tm),
        in_specs=[pl.BlockSpec((1, tm, k), lambda bi, i: (bi, i, 0)),
                  pl.BlockSpec((MLA_HEADS, k, QK_PAD), lambda bi, i: (0, 0, 0)), tab, tab, tab],
        out_specs=pl.BlockSpec((1, MLA_HEADS, tm, QK_PAD), lambda bi, i: (bi, 0, i, 0)),
        out_shape=jax.ShapeDtypeStruct((b, MLA_HEADS, s, QK_PAD), BF16),
        compiler_params=_params("arbitrary", "arbitrary"),
        name="q_up",
    )(cqn, w, *rope)


def _kv_up_compute(a_ref, pe_ref, w_ref, k_ref, v_ref):
    a = a_ref[0]
    pe = pe_ref[0]
    width = QK_NOPE_DIM + V_HEAD_DIM
    ones = jnp.ones((a.shape[0], V_PAD - V_HEAD_DIM), v_ref.dtype)
    for h in range(MLA_HEADS):
        r = _dot(a, w_ref[:, h * width:(h + 1) * width])
        k_ref[0, h, :, :LANES] = r[:, :QK_NOPE_DIM].astype(k_ref.dtype)
        k_ref[0, h, :, LANES:] = pe
        v_ref[0, h, :, :V_HEAD_DIM] = r[:, QK_NOPE_DIM:].astype(v_ref.dtype)
        v_ref[0, h, :, V_HEAD_DIM:] = ones


def _kv_up_kernel(a_ref, pe_ref, w_ref, k_ref, v_ref):
    _kv_up_compute(a_ref, pe_ref, w_ref, k_ref, v_ref)


def _kv_up_prefix_kernel(a_ref, pe_ref, w_ref, kp_ref, vp_ref, k_ref, v_ref):
    @pl.when(pl.program_id(1) == 0)
    def _():
        k_ref[...] = kp_ref[...]
        v_ref[...] = vp_ref[...]

    @pl.when(pl.program_id(1) > 0)
    def _():
        _kv_up_compute(a_ref, pe_ref, w_ref, k_ref, v_ref)


def _kv_up(ckvn, kpe, w_all, layer, tm, prefix=None):
    b, s, k = ckvn.shape
    wspec = pl.BlockSpec((None,) + w_all.shape[1:], lambda bi, i: (layer, 0, 0))
    if prefix is None:
        n_pre, kernel, extra, extra_specs = 0, _kv_up_kernel, (), []
        row = lambda bi, i: (bi, i, 0)
    else:
        n_pre, kernel, extra = 1, _kv_up_prefix_kernel, tuple(prefix)
        assert prefix[0].shape[2] == tm
        row = lambda bi, i: (bi, jnp.maximum(i - 1, 0), 0)
        extra_specs = [pl.BlockSpec((1, MLA_HEADS, tm, QK_PAD), lambda bi, i: (bi, 0, 0, 0)),
                       pl.BlockSpec((1, MLA_HEADS, tm, V_PAD), lambda bi, i: (bi, 0, 0, 0))]
    s_out = s + n_pre * tm
    return pl.pallas_call(
        kernel,
        grid=(b, s_out // tm),
        in_specs=[pl.BlockSpec((1, tm, k), row), pl.BlockSpec((1, tm, LANES), row), wspec] + extra_specs,
        out_specs=[pl.BlockSpec((1, MLA_HEADS, tm, QK_PAD), lambda bi, i: (bi, 0, i, 0)),
                   pl.BlockSpec((1, MLA_HEADS, tm, V_PAD), lambda bi, i: (bi, 0, i, 0))],
        out_shape=[jax.ShapeDtypeStruct((b, MLA_HEADS, s_out, QK_PAD), BF16),
                   jax.ShapeDtypeStruct((b, MLA_HEADS, s_out, V_PAD), BF16)],
        compiler_params=_params("arbitrary", "arbitrary"),
        name="kv_up",
    )(ckvn, kpe, w_all, *extra)


def _flash_init(m_ref, acc_ref):
    m_ref[...] = jnp.full(m_ref.shape, -jnp.inf, F32)
    acc_ref[...] = jnp.zeros(acc_ref.shape, F32)


def _flash_update(q_ref, k, v, m_ref, acc_ref):
    tq = q_ref.shape[2]
    n_blk = k.shape[0] // LANES
    for g in range(tq // FLASH_GROUP_ROWS):
        rows = slice(g * FLASH_GROUP_ROWS, (g + 1) * FLASH_GROUP_ROWS)
        s = lax.dot_general(q_ref[0, 0, rows, :], k, (((1,), (1,)), ((), ())),
                            preferred_element_type=F32)
        blocks = [s[:, c * LANES:(c + 1) * LANES] for c in range(n_blk)]
        m_blk = functools.reduce(jnp.maximum, blocks)
        m_prev = m_ref[rows, :]
        m_new = jnp.maximum(m_prev, jnp.max(m_blk, axis=1, keepdims=True))
        p = jnp.concatenate([jnp.exp2(blk - m_new).astype(BF16) for blk in blocks], axis=1)
        alpha = jnp.exp2(m_prev - m_new)
        acc_ref[rows, :] = acc_ref[rows, :] * jnp.concatenate([alpha, alpha], axis=1) + _dot(p, v)
        m_ref[rows, :] = m_new


def _flash_finish(o_ref, acc_ref):
    acc = acc_ref[...]
    o_ref[0] = (acc[:, :V_HEAD_DIM] / acc[:, V_HEAD_DIM:]).astype(o_ref.dtype)


def _flash_kernel(q_ref, k_ref, v_ref, *rest, tk):
    cast_src, o_ref, cast_dst = _split_riders(rest[:-2])
    m_ref, acc_ref = rest[-2:]
    _flash_init(m_ref, acc_ref)
    for j in range(k_ref.shape[2] // tk):
        rows = slice(j * tk, (j + 1) * tk)
        _flash_update(q_ref, k_ref[0, 0, rows, :], v_ref[0, 0, rows, :], m_ref, acc_ref)
    _flash_finish(o_ref, acc_ref)
    _run_casts(cast_src, cast_dst)


def _kv_chunk(n_kv):
    mxu = 2 * LANES
    return max(t for t in range(mxu, FLASH_MAX_CHUNK + 1, mxu) if n_kv % t == 0)


def _flash(q, k, v, tq, tk, layer, cast=()):
    b, nh, s, _ = q.shape
    n_kv = k.shape[2]
    n_q = s // tq
    whole = lambda w: pl.BlockSpec((1, 1, n_kv, w), lambda bi, h, i: (bi, h, 0, 0))
    c_in, c_out, c_shape = _cast_specs(cast, layer, b * nh * n_q,
                                       lambda bi, h, i: (bi * nh + h) * n_q + i)
    out = pl.pallas_call(
        functools.partial(_flash_kernel, tk=tk),
        grid=(b, nh, n_q),
        in_specs=[pl.BlockSpec((1, 1, tq, QK_PAD), lambda bi, h, i: (bi, h, i, 0)),
                  whole(QK_PAD), whole(V_PAD)] + c_in,
        out_specs=[pl.BlockSpec((1, tq, V_HEAD_DIM), lambda bi, h, i: (bi, i, h))] + c_out,
        out_shape=[jax.ShapeDtypeStruct((b, s, nh * V_HEAD_DIM), BF16)] + c_shape,
        scratch_shapes=[pltpu.VMEM((tq, LANES), F32), pltpu.VMEM((tq, V_PAD), F32)],
        compiler_params=_params("arbitrary", "arbitrary", "arbitrary"),
        name="flash",
    )(q, k, v, *cast)
    return out[0], tuple(out[1:])


def _conv_kernel(bg_ref, cg_ref, xin_ref, cgp_ref, xinp_ref, cgn_ref, xinn_ref, w_ref, o_ref):
    i = pl.program_id(1)
    tm = cg_ref.shape[1]
    u = cg_ref[0].astype(F32) * xin_ref[0].astype(F32)
    u_prev = cgp_ref[0, HALO - 1:HALO, :].astype(F32) * xinp_ref[0, HALO - 1:HALO, :].astype(F32)
    u_next = cgn_ref[0, 0:1, :].astype(F32) * xinn_ref[0, 0:1, :].astype(F32)
    u_prev = jnp.where(i > 0, u_prev, 0.0)
    u_next = jnp.where(i < pl.num_programs(1) - 1, u_next, 0.0)
    row = lax.broadcasted_iota(jnp.int32, u.shape, 0)
    below = jnp.where(row == 0, u_prev, pltpu.roll(u, 1, 0))
    above = jnp.where(row == tm - 1, u_next, pltpu.roll(u, tm - 1, 0))
    y = below * w_ref[0:1, :] + u * w_ref[1:2, :] + above * w_ref[2:3, :]
    o_ref[0] = (bg_ref[0].astype(F32) * y).astype(o_ref.dtype)


def _conv_branch(f_conv, conv_w, tm):
    b, s, _ = f_conv.shape
    c = CONV_DIM
    hb = tm // HALO
    n_hb = s // HALO
    main = lambda col: pl.BlockSpec((1, tm, c), lambda bi, i: (bi, i, col))
    prev = lambda col: pl.BlockSpec((1, HALO, c), lambda bi, i: (bi, jnp.maximum(i * hb - 1, 0), col))
    nxt = lambda col: pl.BlockSpec((1, HALO, c), lambda bi, i: (bi, jnp.minimum((i + 1) * hb, n_hb - 1), col))
    return pl.pallas_call(
        _conv_kernel,
        grid=(b, s // tm),
        in_specs=[main(0), main(1), main(2), prev(1), prev(2), nxt(1), nxt(2),
                  pl.BlockSpec(conv_w.shape, lambda bi, i: (0, 0))],
        out_specs=pl.BlockSpec((1, tm, c), lambda bi, i: (bi, i, 0)),
        out_shape=jax.ShapeDtypeStruct((b, s, c), BF16),
        compiler_params=_params("arbitrary", "arbitrary"),
        name="conv",
    )(f_conv, f_conv, f_conv, f_conv, f_conv, f_conv, f_conv, conv_w)


def _pool_kernel(u_ref, up_ref, un_ref, w_ref, sc_ref, o_ref, *, seq_len):
    i = pl.program_id(1)
    tm = u_ref.shape[1]
    rows = tm + 2 * HALO
    prev = jnp.where(i > 0, up_ref[0].astype(F32), 0.0)
    nxt = jnp.where(i < pl.num_programs(1) - 1, un_ref[0].astype(F32), 0.0)
    ext = jnp.concatenate([prev, u_ref[0].astype(F32), nxt], axis=0)
    t = i * tm + lax.broadcasted_iota(jnp.int32, (tm, 1), 0)
    g = POOL_GROUP_DIM
    for gi, win in enumerate(POOL_WINDOWS):
        e = ext[:, gi * g:(gi + 1) * g]
        acc = e + pltpu.roll(e, 1, 0)
        half = 1
        while 2 * half < win:
            acc = pltpu.roll(acc, half, 0) + pltpu.roll(acc, rows - half, 0)
            half *= 2
        lo = jnp.maximum(t - win // 2, 0)
        hi = jnp.minimum(t + (win - win // 2), seq_len)
        cnt = (hi - lo).astype(F32)
        p = acc[HALO:HALO + tm] / cnt - e[HALO:HALO + tm]
        y = _dot(p.astype(BF16), w_ref[gi]) * sc_ref[:, gi * g:(gi + 1) * g]
        o_ref[0, :, gi * g:(gi + 1) * g] = y.astype(o_ref.dtype)


def _pool_branch(f_pool, pool_w_all, layer, pool_scale, tm):
    b, s, c = f_pool.shape
    hb = tm // HALO
    n_hb = s // HALO
    return pl.pallas_call(
        functools.partial(_pool_kernel, seq_len=s),
        grid=(b, s // tm),
        in_specs=[pl.BlockSpec((1, tm, c), lambda bi, i: (bi, i, 0)),
                  pl.BlockSpec((1, HALO, c), lambda bi, i: (bi, jnp.maximum(i * hb - 1, 0), 0)),
                  pl.BlockSpec((1, HALO, c), lambda bi, i: (bi, jnp.minimum((i + 1) * hb, n_hb - 1), 0)),
                  pl.BlockSpec((None,) + pool_w_all.shape[1:], lambda bi, i: (layer, 0, 0, 0)),
                  pl.BlockSpec((1, c), lambda bi, i: (0, 0))],
        out_specs=pl.BlockSpec((1, tm, c), lambda bi, i: (bi, i, 0)),
        out_shape=jax.ShapeDtypeStruct((b, s, c), BF16),
        compiler_params=_params("arbitrary", "arbitrary"),
        name="pool",
    )(f_pool, f_pool, f_pool, pool_w_all, pool_scale.reshape(1, c))


def _merge_kernel(h_ref, cy_ref, ay_ref, py_ref, wg0_ref, wg1_ref, wg2_ref, wc_ref, wa_ref, wp_ref,
                  o_ref):
    h = h_ref[...]
    out = _sigmoid(_dot(h, wg0_ref[...])) * _dot(cy_ref[...], wc_ref[...])
    out += _sigmoid(_dot(h, wg1_ref[...])) * _dot(ay_ref[...], wa_ref[...])
    out += _sigmoid(_dot(h, wg2_ref[...])) * _dot(py_ref[...], wp_ref[...])
    o_ref[...] = out.astype(o_ref.dtype)


def _merge(h, conv_y, attn_y, pool_y, w_gate, w_conv_out, w_mla_out, w_pool_out, layer, tm, tn):
    m, d = h.shape
    n = w_conv_out.shape[2]
    n_tiles = n // tn
    row = lambda a: pl.BlockSpec((tm, a.shape[1]), lambda i, j: (i, 0))
    col = lambda w: pl.BlockSpec((None, w.shape[1], tn), lambda i, j: (layer, 0, j))
    gate = lambda br: pl.BlockSpec((d, tn), lambda i, j: (0, br * n_tiles + j))
    return pl.pallas_call(
        _merge_kernel,
        grid=(m // tm, n_tiles),
        in_specs=[row(h), row(conv_y), row(attn_y), row(pool_y), gate(0), gate(1), gate(2),
                  col(w_conv_out), col(w_mla_out), col(w_pool_out)],
        out_specs=pl.BlockSpec((tm, tn), lambda i, j: (i, j)),
        out_shape=jax.ShapeDtypeStruct((m, n), BF16),
        compiler_params=_params("arbitrary", "arbitrary"),
        name="merge",
    )(h, conv_y, attn_y, pool_y, w_gate, w_gate, w_gate, w_conv_out, w_mla_out, w_pool_out)


def _mm_resid_kernel(a_ref, w_ref, x_ref, g_ref, o_ref):
    o_ref[0] = x_ref[0] + g_ref[0] * _dot(a_ref[0], w_ref[...])


def _matmul_resid(a, w, layer, x, g, tm, tn, name):
    b, s, k = a.shape
    n = w.shape[-1]
    if w.ndim == 3:
        wspec = pl.BlockSpec((None, k, tn), lambda bi, i, j: (layer, 0, j))
    else:
        wspec = pl.BlockSpec((k, tn), lambda bi, i, j: (0, j))
    return pl.pallas_call(
        _mm_resid_kernel,
        grid=(b, s // tm, n // tn),
        in_specs=[pl.BlockSpec((1, tm, k), lambda bi, i, j: (bi, i, 0)), wspec,
                  pl.BlockSpec((1, tm, tn), lambda bi, i, j: (bi, i, j)),
                  pl.BlockSpec((1, 1, tn), lambda bi, i, j: (bi, 0, j))],
        out_specs=pl.BlockSpec((1, tm, tn), lambda bi, i, j: (bi, i, j)),
        out_shape=jax.ShapeDtypeStruct((b, s, n), F32),
        compiler_params=_params("arbitrary", "arbitrary", "arbitrary"),
        name=name,
    )(a, w, x, g)


def _ffn_up_kernel(a_ref, wg_ref, wu_ref, *rest):
    cast_src, o_ref, cast_dst = _split_riders(rest)
    a = a_ref[...]
    gate = _dot(a, wg_ref[...])
    o_ref[...] = (gate * _sigmoid(gate) * _dot(a, wu_ref[...])).astype(o_ref.dtype)
    _run_casts(cast_src, cast_dst)


def _ffn_up(h, w_gate, w_up, layer, tm, tn, cast=()):
    m, k = h.shape
    n = w_gate.shape[1]
    n_tiles = n // tn
    wspec = pl.BlockSpec((k, tn), lambda i, j: (0, j))
    c_in, c_out, c_shape = _cast_specs(cast, layer, (m // tm) * n_tiles, lambda i, j: i * n_tiles + j)
    out = pl.pallas_call(
        _ffn_up_kernel,
        grid=(m // tm, n_tiles),
        in_specs=[pl.BlockSpec((tm, k), lambda i, j: (i, 0)), wspec, wspec] + c_in,
        out_specs=[pl.BlockSpec((tm, tn), lambda i, j: (i, j))] + c_out,
        out_shape=[jax.ShapeDtypeStruct((m, n), BF16)] + c_shape,
        compiler_params=_params("arbitrary", "arbitrary"),
        name="ffn_up",
    )(h, w_gate, w_up, *cast)
    return out[0], tuple(out[1:])


def _rope_tables(n_tokens, rotate):
    lane = jnp.arange(LANES)
    valid = lane < QK_ROPE_DIM
    if not rotate:
        cos = jnp.broadcast_to(valid.astype(F32), (n_tokens, LANES))
        zero = jnp.zeros((n_tokens, LANES), F32)
        return cos, zero, zero
    axis_dim = QK_ROPE_DIM // 2
    t = jnp.arange(n_tokens, dtype=jnp.int32)
    pos = jnp.stack([t // GRID_W, t % GRID_W], axis=1).astype(F32)
    inv_freq = 1.0 / (ROPE_THETA ** (jnp.arange(0, axis_dim, 2, dtype=F32) / axis_dim))
    axis = jnp.clip(lane // axis_dim, 0, 1)
    upper = (lane % axis_dim) >= axis_dim // 2
    ang = pos[:, axis] * inv_freq[lane % (axis_dim // 2)][None, :]
    cos = jnp.where(valid, jnp.cos(ang), 0.0)
    sin = jnp.sin(ang)
    sin_lo = jnp.where(valid & ~upper, -sin, 0.0)
    sin_hi = jnp.where(valid & upper, sin, 0.0)
    return cos, sin_lo, sin_hi


def _layer_weights(l, w_in, w_uq, w_ukv, w_conv_out, w_mla_out, w_pool_out, w_o, pool_w):
    d = D_MODEL
    wi = w_in[l]
    w_kv = jnp.pad(wi[:, COL_KV:COL_POOL], ((0, 0), (0, LANES - QK_ROPE_DIM)))
    wq = w_uq[l].reshape(Q_LORA_RANK, MLA_HEADS, QK_NOPE_DIM + QK_ROPE_DIM)
    wq = jnp.pad(wq, ((0, 0), (0, 0), (0, QK_PAD - QK_NOPE_DIM - QK_ROPE_DIM))).transpose(1, 0, 2)
    return dict(
        conv=wi[:, :COL_Q].astype(BF16),
        cq=wi[:, COL_Q:COL_KV].astype(BF16),
        ckv=w_kv.astype(BF16),
        pool=wi[:, COL_POOL:COL_GATE].astype(BF16),
        gate=wi[:, COL_GATE:].astype(BF16),
        uq=wq.astype(BF16),
        ukv=w_ukv.astype(BF16),
        conv_out=w_conv_out.astype(BF16),
        mla_out=w_mla_out.astype(BF16),
        pool_out=w_pool_out.astype(BF16),
        o=w_o.astype(BF16),
        pool_w=pool_w.astype(BF16),
    )


def _kv_path(h2d, b, s, w, layer, kv_norm_w, rope, tm, prefix=None):
    ckvn, kpe = _ckv_proj(h2d, w["ckv"], kv_norm_w, rope, tm)
    tm_kv = tm if prefix is None else prefix[0].shape[2]
    return _kv_up(ckvn.reshape(b, s, -1), kpe.reshape(b, s, -1), w["ukv"], layer, tm_kv, prefix)


def kernel(x, c, ctx, c_ctx, w_mod, b_mod, norm_mix_w, norm_ffn_w, w_in, conv_w, w_conv_out,
           q_norm_w, kv_norm_w, w_uq, w_ukv, w_mla_out, pool_w, pool_scale, w_pool_out, w_o,
           w_ffn_gate, w_ffn_up, w_ffn_down, final_norm_w):
    b, n_lat, d = x.shape
    n_ctx = ctx.shape[1]
    rope_lat = _rope_tables(n_lat, True)
    rope_ctx = _rope_tables(n_ctx, False)

    cvec = jnp.concatenate([c, c_ctx[None, :], jnp.zeros((SUBLANES - b - 1, d), F32)], axis=0)
    mod_all = _mod(cvec, w_mod, b_mod).reshape(DEPTH, SUBLANES, N_MOD, d)

    tm_lat = 512
    tm_ctx = n_ctx
    h_ctx = ctx
    for l in range(DEPTH):
        last = l == DEPTH - 1
        w = _layer_weights(l, w_in, w_uq, w_ukv, w_conv_out, w_mla_out, w_pool_out, w_o, pool_w)
        mod_lat = [mod_all[l, :b, i].reshape(b, 1, d) for i in range(N_MOD)]
        mod_ctx = [jnp.broadcast_to(mod_all[l, b, i].reshape(1, 1, d), (b, 1, d)) for i in range(N_MOD)]

        def front(stream, mod, s, rope, tm, kv_prefix):
            h = _norm_mod(stream, norm_mix_w[l], mod[0], mod[1], tm)
            h2d = h.reshape(b * s, d)
            keys, values = _kv_path(h2d, b, s, w, l, kv_norm_w[l], rope, min(2 * tm, s), kv_prefix)
            return h2d, keys, values

        def back(stream, h2d, keys, values, mod, s, rope, tm, ffn_w):
            _, _, g1, sh2, sc2, g2 = mod
            tm_mm = 2 * tm
            tm_res = min(2 * tm, s)
            cqn = _cq_proj(h2d, w["cq"], q_norm_w[l], tm_res)
            q = _q_up(cqn.reshape(b, s, -1), w["uq"], rope, tm)
            attn, up_w = _flash(q, keys, values, min(s, 2048), _kv_chunk(keys.shape[2]), l,
                                cast=(w_ffn_gate, w_ffn_up) if ffn_w is None else ())
            w_gate_b, w_up_b = up_w if ffn_w is None else ffn_w[:2]
            f_conv = _matmul(h2d, w["conv"], tm_mm, 1024, "in_conv")
            conv_y = _conv_branch(f_conv.reshape(b, s, -1), conv_w[l], tm)
            f_pool = _matmul(h2d, w["pool"], tm_mm, 1024, "in_pool")
            pool_y = _pool_branch(f_pool.reshape(b, s, -1), w["pool_w"], l, pool_scale[l], tm)
            merged = _merge(h2d, conv_y.reshape(b * s, -1), attn.reshape(b * s, -1),
                            pool_y.reshape(b * s, -1), w["gate"], w["conv_out"], w["mla_out"],
                            w["pool_out"], l, tm, 512)
            x1 = _matmul_resid(merged.reshape(b, s, d), w["o"], l, stream, g1, tm_res, 512, "w_o")
            h_ffn = _norm_mod(x1, norm_ffn_w[l], sh2, sc2, tm)
            hid, down_w = _ffn_up(h_ffn.reshape(b * s, d), w_gate_b, w_up_b, l, min(4 * tm, b * s), 256,
                                  cast=(w_ffn_down,) if ffn_w is None else ())
            w_down_b = down_w[0] if ffn_w is None else ffn_w[2]
            x2 = _matmul_resid(hid.reshape(b, s, -1), w_down_b, l, x1, g2, tm, 512, "ffn_down")
            return x2, (w_gate_b, w_up_b, w_down_b)

        hc2d, keys_c, values_c = front(h_ctx, mod_ctx, n_ctx, rope_ctx, tm_ctx, None)
        h2d, keys, values = front(x, mod_lat, n_lat, rope_lat, tm_lat, (keys_c, values_c))
        x, ffn_w = back(x, h2d, keys, values, mod_lat, n_lat, rope_lat, tm_lat, None)
        if not last:
            h_ctx, _ = back(h_ctx, hc2d, keys_c, values_c, mod_ctx, n_ctx, rope_ctx, tm_ctx, ffn_w)

    return _final_norm(x.reshape(b * n_lat, d), final_norm_w, 512).reshape(b, n_lat, d)
```

```python
import functools

import jax
import jax.numpy as jnp
from jax import lax
from jax.experimental import pallas as pl
from jax.experimental.pallas import tpu as pltpu

F32 = jnp.float32
BF16 = jnp.bfloat16

D_MODEL = 4096
DEPTH = 2
GRID_W = 64
CONV_DIM = 1024
MLA_HEADS = 16
QK_NOPE_DIM = 128
QK_ROPE_DIM = 64
V_HEAD_DIM = 128
Q_LORA_RANK = 1024
KV_LORA_RANK = 512
ROPE_THETA = 10000.0
POOL_WINDOWS = (2, 4, 8, 16)
POOL_GROUP_DIM = 256
POOL_DIM = len(POOL_WINDOWS) * POOL_GROUP_DIM
N_BRANCHES = 3
N_MOD = 6
NORM_EPS = 1e-6

COL_Q = 3 * CONV_DIM
COL_KV = COL_Q + Q_LORA_RANK
COL_KPE = COL_KV + KV_LORA_RANK
COL_POOL = COL_KPE + QK_ROPE_DIM
COL_GATE = COL_POOL + POOL_DIM

LANES = 128
SUBLANES = 8
QK_PAD = 2 * LANES
BF16_SUBLANES = 16
HALO = BF16_SUBLANES
VMEM_LIMIT = 56 * 1024 * 1024
V_PAD = 2 * LANES
FLASH_GROUP_ROWS = 256
FLASH_MAX_CHUNK = 3072
Q_SCALE = (QK_NOPE_DIM + QK_ROPE_DIM) ** -0.5 * 1.4426950408889634


def _params(*sem):
    return pltpu.CompilerParams(dimension_semantics=sem, vmem_limit_bytes=VMEM_LIMIT)


def _sigmoid(x):
    return 1.0 / (1.0 + jnp.exp(-x))


def _dot(a, b):
    return jnp.dot(a, b, preferred_element_type=F32)


def _rms(x, w):
    ms = jnp.mean(x * x, axis=-1, keepdims=True)
    return x * lax.rsqrt(ms + NORM_EPS) * w


def _rope(pe, cos, sin_lo, sin_hi):
    return (pe * cos + pltpu.roll(pe, LANES - 16, 1) * sin_lo
            + pltpu.roll(pe, 16, 1) * sin_hi)


def _cast_specs(casts, n_steps, step):
    ins, outs, shapes = [], [], []
    for src, layer in casts:
        _, rows, cols = src.shape
        assert rows % n_steps == 0 and (rows // n_steps) % BF16_SUBLANES == 0
        r = rows // n_steps
        ins.append(pl.BlockSpec((None, r, cols), lambda *g, layer=layer: (layer, step(*g), 0)))
        outs.append(pl.BlockSpec((r, cols), lambda *g: (step(*g), 0)))
        shapes.append(jax.ShapeDtypeStruct((rows, cols), BF16))
    return ins, outs, shapes


def _split_riders(rest):
    n = (len(rest) - 1) // 2
    return rest[:n], rest[n], rest[n + 1:]


def _run_casts(srcs, dsts):
    for src, dst in zip(srcs, dsts):
        dst[...] = src[...].astype(dst.dtype)


def _mod_kernel(c_ref, b_ref, *refs):
    w_refs, o_ref = refs[:-1], refs[-1]
    c = c_ref[...]
    a = (c * _sigmoid(c)).astype(BF16)
    tn = w_refs[0].shape[2]
    for i, w_ref in enumerate(w_refs):
        cols = slice(i * tn, (i + 1) * tn)
        o_ref[0, :, cols] = _dot(a, w_ref[0].astype(BF16)) + b_ref[0, :, cols]


def _mod(cvec, w_mod, b_mod):
    depth, d, n = w_mod.shape
    rows = cvec.shape[0]
    tn, n_streams = 256, 4
    step = tn * n_streams
    wspec = lambda i: pl.BlockSpec((1, d, tn), lambda l, j: (l, 0, j * n_streams + i))
    return pl.pallas_call(
        _mod_kernel,
        grid=(depth, n // step),
        in_specs=[pl.BlockSpec((rows, d), lambda l, j: (0, 0)),
                  pl.BlockSpec((1, 1, step), lambda l, j: (l, 0, j))]
                 + [wspec(i) for i in range(n_streams)],
        out_specs=pl.BlockSpec((1, rows, step), lambda l, j: (l, 0, j)),
        out_shape=jax.ShapeDtypeStruct((depth, rows, n), F32),
        compiler_params=_params("arbitrary", "arbitrary"),
        name="mod",
    )(cvec, b_mod.reshape(depth, 1, n), *([w_mod] * n_streams))


def _norm_mod_kernel(x_ref, w_ref, sh_ref, sc_ref, o_ref):
    y = _rms(x_ref[0], w_ref[...])
    o_ref[0] = (y * (1.0 + sc_ref[0]) + sh_ref[0]).astype(o_ref.dtype)


def _norm_mod(x, w, shift, scale, tm):
    b, s, d = x.shape
    vec = pl.BlockSpec((1, 1, d), lambda bi, i: (bi, 0, 0))
    return pl.pallas_call(
        _norm_mod_kernel,
        grid=(b, s // tm),
        in_specs=[pl.BlockSpec((1, tm, d), lambda bi, i: (bi, i, 0)),
                  pl.BlockSpec((1, d), lambda bi, i: (0, 0)), vec, vec],
        out_specs=pl.BlockSpec((1, tm, d), lambda bi, i: (bi, i, 0)),
        out_shape=jax.ShapeDtypeStruct((b, s, d), BF16),
        compiler_params=_params("arbitrary", "arbitrary"),
        name="norm_mod",
    )(x, w.reshape(1, d), shift, scale)


def _norm_kernel(x_ref, w_ref, o_ref):
    o_ref[...] = _rms(x_ref[...], w_ref[...])


def _final_norm(x, w, tm):
    m, d = x.shape
    return pl.pallas_call(
        _norm_kernel,
        grid=(m // tm,),
        in_specs=[pl.BlockSpec((tm, d), lambda i: (i, 0)),
                  pl.BlockSpec((1, d), lambda i: (0, 0))],
        out_specs=pl.BlockSpec((tm, d), lambda i: (i, 0)),
        out_shape=jax.ShapeDtypeStruct((m, d), F32),
        compiler_params=_params("arbitrary"),
        name="final_norm",
    )(x, w.reshape(1, d))


def _mm_kernel(a_ref, w_ref, o_ref):
    o_ref[...] = _dot(a_ref[...], w_ref[...]).astype(o_ref.dtype)


def _matmul(a, w, tm, tn, name):
    m, k = a.shape
    n = w.shape[1]
    return pl.pallas_call(
        _mm_kernel,
        grid=(m // tm, n // tn),
        in_specs=[pl.BlockSpec((tm, k), lambda i, j: (i, 0)),
                  pl.BlockSpec((k, tn), lambda i, j: (0, j))],
        out_specs=pl.BlockSpec((tm, tn), lambda i, j: (i, j)),
        out_shape=jax.ShapeDtypeStruct((m, n), BF16),
        compiler_params=_params("arbitrary", "arbitrary"),
        name=name,
    )(a, w)


def _cq_kernel(a_ref, w_ref, nw_ref, o_ref):
    o_ref[...] = _rms(_dot(a_ref[...], w_ref[...]), nw_ref[...]).astype(o_ref.dtype)


def _cq_proj(h, w, nw, tm):
    m, k = h.shape
    n = w.shape[1]
    return pl.pallas_call(
        _cq_kernel,
        grid=(m // tm,),
        in_specs=[pl.BlockSpec((tm, k), lambda i: (i, 0)),
                  pl.BlockSpec((k, n), lambda i: (0, 0)),
                  pl.BlockSpec((1, n), lambda i: (0, 0))],
        out_specs=pl.BlockSpec((tm, n), lambda i: (i, 0)),
        out_shape=jax.ShapeDtypeStruct((m, n), BF16),
        compiler_params=_params("arbitrary"),
        name="cq_proj",
    )(h, w, nw.reshape(1, n))


def _ckv_kernel(a_ref, w_ref, nw_ref, cos_ref, slo_ref, shi_ref, ckv_ref, kpe_ref):
    r = _dot(a_ref[...], w_ref[...])
    ckv_ref[...] = _rms(r[:, :KV_LORA_RANK], nw_ref[...]).astype(ckv_ref.dtype)
    pe = _rope(r[:, KV_LORA_RANK:], cos_ref[...], slo_ref[...], shi_ref[...])
    kpe_ref[...] = pe.astype(kpe_ref.dtype)


def _ckv_proj(h, w, nw, rope, tm):
    m, k = h.shape
    n = w.shape[1]
    n_pos = rope[0].shape[0] // tm
    tab = pl.BlockSpec((tm, LANES), lambda i: (i % n_pos, 0))
    return pl.pallas_call(
        _ckv_kernel,
        grid=(m // tm,),
        in_specs=[pl.BlockSpec((tm, k), lambda i: (i, 0)),
                  pl.BlockSpec((k, n), lambda i: (0, 0)),
                  pl.BlockSpec((1, KV_LORA_RANK), lambda i: (0, 0)), tab, tab, tab],
        out_specs=[pl.BlockSpec((tm, KV_LORA_RANK), lambda i: (i, 0)),
                   pl.BlockSpec((tm, LANES), lambda i: (i, 0))],
        out_shape=[jax.ShapeDtypeStruct((m, KV_LORA_RANK), BF16),
                   jax.ShapeDtypeStruct((m, LANES), BF16)],
        compiler_params=_params("arbitrary"),
        name="ckv_proj",
    )(h, w, nw.reshape(1, KV_LORA_RANK), *rope)


def _q_up_kernel(a_ref, w_ref, cos_ref, slo_ref, shi_ref, q_ref):
    a = a_ref[0]
    cos, slo, shi = cos_ref[...], slo_ref[...], shi_ref[...]
    for h in range(MLA_HEADS):
        r = _dot(a, w_ref[h])
        q_ref[0, h, :, :LANES] = (r[:, :LANES] * Q_SCALE).astype(q_ref.dtype)
        pe = _rope(r[:, LANES:], cos, slo, shi)
        q_ref[0, h, :, LANES:] = (pe * Q_SCALE).astype(q_ref.dtype)


def _q_up(cqn, w, rope, tm):
    b, s, k = cqn.shape
    tab = pl.BlockSpec((tm, LANES), lambda bi, i: (i, 0))
    return pl.pallas_call(
        _q_up_kernel,
        grid=(b, s // tm),
        in_specs=[pl.BlockSpec((1, tm, k), lambda bi, i: (bi, i, 0)),
                  pl.BlockSpec((MLA_HEADS, k, QK_PAD), lambda bi, i: (0, 0, 0)), tab, tab, tab],
        out_specs=pl.BlockSpec((1, MLA_HEADS, tm, QK_PAD), lambda bi, i: (bi, 0, i, 0)),
        out_shape=jax.ShapeDtypeStruct((b, MLA_HEADS, s, QK_PAD), BF16),
        compiler_params=_params("arbitrary", "arbitrary"),
        name="q_up",
    )(cqn, w, *rope)


def _kv_up_compute(a_ref, pe_ref, w_ref, k_ref, v_ref):
    a = a_ref[0]
    pe = pe_ref[0]
    width = QK_NOPE_DIM + V_HEAD_DIM
    ones = jnp.ones((a.shape[0], V_PAD - V_HEAD_DIM), v_ref.dtype)
    for h in range(MLA_HEADS):
        r = _dot(a, w_ref[:, h * width:(h + 1) * width])
        k_ref[0, h, :, :LANES] = r[:, :QK_NOPE_DIM].astype(k_ref.dtype)
        k_ref[0, h, :, LANES:] = pe
        v_ref[0, h, :, :V_HEAD_DIM] = r[:, QK_NOPE_DIM:].astype(v_ref.dtype)
        v_ref[0, h, :, V_HEAD_DIM:] = ones


def _kv_up_kernel(a_ref, pe_ref, w_ref, k_ref, v_ref):
    _kv_up_compute(a_ref, pe_ref, w_ref, k_ref, v_ref)


def _kv_up_prefix_kernel(a_ref, pe_ref, w_ref, kp_ref, vp_ref, k_ref, v_ref):
    @pl.when(pl.program_id(1) == 0)
    def _():
        k_ref[...] = kp_ref[...]
        v_ref[...] = vp_ref[...]

    @pl.when(pl.program_id(1) > 0)
    def _():
        _kv_up_compute(a_ref, pe_ref, w_ref, k_ref, v_ref)


def _kv_up(ckvn, kpe, w_all, layer, tm, prefix=None):
    b, s, k = ckvn.shape
    wspec = pl.BlockSpec((None,) + w_all.shape[1:], lambda bi, i: (layer, 0, 0))
    if prefix is None:
        n_pre, kernel, extra, extra_specs = 0, _kv_up_kernel, (), []
        row = lambda bi, i: (bi, i, 0)
    else:
        n_pre, kernel, extra = 1, _kv_up_prefix_kernel, tuple(prefix)
        assert prefix[0].shape[2] == tm
        row = lambda bi, i: (bi, jnp.maximum(i - 1, 0), 0)
        extra_specs = [pl.BlockSpec((1, MLA_HEADS, tm, QK_PAD), lambda bi, i: (bi, 0, 0, 0)),
                       pl.BlockSpec((1, MLA_HEADS, tm, V_PAD), lambda bi, i: (bi, 0, 0, 0))]
    s_out = s + n_pre * tm
    return pl.pallas_call(
        kernel,
        grid=(b, s_out // tm),
        in_specs=[pl.BlockSpec((1, tm, k), row), pl.BlockSpec((1, tm, LANES), row), wspec] + extra_specs,
        out_specs=[pl.BlockSpec((1, MLA_HEADS, tm, QK_PAD), lambda bi, i: (bi, 0, i, 0)),
                   pl.BlockSpec((1, MLA_HEADS, tm, V_PAD), lambda bi, i: (bi, 0, i, 0))],
        out_shape=[jax.ShapeDtypeStruct((b, MLA_HEADS, s_out, QK_PAD), BF16),
                   jax.ShapeDtypeStruct((b, MLA_HEADS, s_out, V_PAD), BF16)],
        compiler_params=_params("arbitrary", "arbitrary"),
        name="kv_up",
    )(ckvn, kpe, w_all, *extra)


def _flash_init(m_ref, acc_ref):
    m_ref[...] = jnp.full(m_ref.shape, -jnp.inf, F32)
    acc_ref[...] = jnp.zeros(acc_ref.shape, F32)


def _flash_update(q_ref, k, v, m_ref, acc_ref):
    tq = q_ref.shape[2]
    n_blk = k.shape[0] // LANES
    for g in range(tq // FLASH_GROUP_ROWS):
        rows = slice(g * FLASH_GROUP_ROWS, (g + 1) * FLASH_GROUP_ROWS)
        s = lax.dot_general(q_ref[0, 0, rows, :], k, (((1,), (1,)), ((), ())),
                            preferred_element_type=F32)
        blocks = [s[:, c * LANES:(c + 1) * LANES] for c in range(n_blk)]
        m_blk = functools.reduce(jnp.maximum, blocks)
        m_prev = m_ref[rows, :]
        m_new = jnp.maximum(m_prev, jnp.max(m_blk, axis=1, keepdims=True))
        p = jnp.concatenate([jnp.exp2(blk - m_new).astype(BF16) for blk in blocks], axis=1)
        alpha = jnp.exp2(m_prev - m_new)
        acc_ref[rows, :] = acc_ref[rows, :] * jnp.concatenate([alpha, alpha], axis=1) + _dot(p, v)
        m_ref[rows, :] = m_new


def _flash_finish(o_ref, acc_ref):
    acc = acc_ref[...]
    o_ref[0] = (acc[:, :V_HEAD_DIM] / acc[:, V_HEAD_DIM:]).astype(o_ref.dtype)


def _flash_kernel(q_ref, k_ref, v_ref, *rest, tk):
    cast_src, o_ref, cast_dst = _split_riders(rest[:-2])
    m_ref, acc_ref = rest[-2:]
    _flash_init(m_ref, acc_ref)
    for j in range(k_ref.shape[2] // tk):
        rows = slice(j * tk, (j + 1) * tk)
        _flash_update(q_ref, k_ref[0, 0, rows, :], v_ref[0, 0, rows, :], m_ref, acc_ref)
    _flash_finish(o_ref, acc_ref)
    _run_casts(cast_src, cast_dst)


def _kv_chunk(n_kv):
    mxu = 2 * LANES
    return max(t for t in range(mxu, FLASH_MAX_CHUNK + 1, mxu) if n_kv % t == 0)


def _flash(q, k, v, tq, tk, cast=()):
    b, nh, s, _ = q.shape
    n_kv = k.shape[2]
    n_q = s // tq
    whole = lambda w: pl.BlockSpec((1, 1, n_kv, w), lambda bi, h, i: (bi, h, 0, 0))
    c_in, c_out, c_shape = _cast_specs(cast, b * nh * n_q, lambda bi, h, i: (bi * nh + h) * n_q + i)
    out = pl.pallas_call(
        functools.partial(_flash_kernel, tk=tk),
        grid=(b, nh, n_q),
        in_specs=[pl.BlockSpec((1, 1, tq, QK_PAD), lambda bi, h, i: (bi, h, i, 0)),
                  whole(QK_PAD), whole(V_PAD)] + c_in,
        out_specs=[pl.BlockSpec((1, tq, V_HEAD_DIM), lambda bi, h, i: (bi, i, h))] + c_out,
        out_shape=[jax.ShapeDtypeStruct((b, s, nh * V_HEAD_DIM), BF16)] + c_shape,
        scratch_shapes=[pltpu.VMEM((tq, LANES), F32), pltpu.VMEM((tq, V_PAD), F32)],
        compiler_params=_params("arbitrary", "arbitrary", "arbitrary"),
        name="flash",
    )(q, k, v, *[src for src, _ in cast])
    return out[0], tuple(out[1:])


def _conv_kernel(bg_ref, cg_ref, xin_ref, cgp_ref, xinp_ref, cgn_ref, xinn_ref, w_ref, o_ref):
    i = pl.program_id(1)
    tm = cg_ref.shape[1]
    u = cg_ref[0].astype(F32) * xin_ref[0].astype(F32)
    u_prev = cgp_ref[0, HALO - 1:HALO, :].astype(F32) * xinp_ref[0, HALO - 1:HALO, :].astype(F32)
    u_next = cgn_ref[0, 0:1, :].astype(F32) * xinn_ref[0, 0:1, :].astype(F32)
    u_prev = jnp.where(i > 0, u_prev, 0.0)
    u_next = jnp.where(i < pl.num_programs(1) - 1, u_next, 0.0)
    row = lax.broadcasted_iota(jnp.int32, u.shape, 0)
    below = jnp.where(row == 0, u_prev, pltpu.roll(u, 1, 0))
    above = jnp.where(row == tm - 1, u_next, pltpu.roll(u, tm - 1, 0))
    y = below * w_ref[0:1, :] + u * w_ref[1:2, :] + above * w_ref[2:3, :]
    o_ref[0] = (bg_ref[0].astype(F32) * y).astype(o_ref.dtype)


def _conv_branch(f_conv, conv_w, tm):
    b, s, _ = f_conv.shape
    c = CONV_DIM
    hb = tm // HALO
    n_hb = s // HALO
    main = lambda col: pl.BlockSpec((1, tm, c), lambda bi, i: (bi, i, col))
    prev = lambda col: pl.BlockSpec((1, HALO, c), lambda bi, i: (bi, jnp.maximum(i * hb - 1, 0), col))
    nxt = lambda col: pl.BlockSpec((1, HALO, c), lambda bi, i: (bi, jnp.minimum((i + 1) * hb, n_hb - 1), col))
    return pl.pallas_call(
        _conv_kernel,
        grid=(b, s // tm),
        in_specs=[main(0), main(1), main(2), prev(1), prev(2), nxt(1), nxt(2),
                  pl.BlockSpec(conv_w.shape, lambda bi, i: (0, 0))],
        out_specs=pl.BlockSpec((1, tm, c), lambda bi, i: (bi, i, 0)),
        out_shape=jax.ShapeDtypeStruct((b, s, c), BF16),
        compiler_params=_params("arbitrary", "arbitrary"),
        name="conv",
    )(f_conv, f_conv, f_conv, f_conv, f_conv, f_conv, f_conv, conv_w)


def _pool_kernel(u_ref, up_ref, un_ref, w_ref, sc_ref, o_ref, *, seq_len):
    i = pl.program_id(1)
    tm = u_ref.shape[1]
    rows = tm + 2 * HALO
    prev = jnp.where(i > 0, up_ref[0].astype(F32), 0.0)
    nxt = jnp.where(i < pl.num_programs(1) - 1, un_ref[0].astype(F32), 0.0)
    ext = jnp.concatenate([prev, u_ref[0].astype(F32), nxt], axis=0)
    t = i * tm + lax.broadcasted_iota(jnp.int32, (tm, 1), 0)
    g = POOL_GROUP_DIM
    for gi, win in enumerate(POOL_WINDOWS):
        e = ext[:, gi * g:(gi + 1) * g]
        acc = e + pltpu.roll(e, 1, 0)
        half = 1
        while 2 * half < win:
            acc = pltpu.roll(acc, half, 0) + pltpu.roll(acc, rows - half, 0)
            half *= 2
        lo = jnp.maximum(t - win // 2, 0)
        hi = jnp.minimum(t + (win - win // 2), seq_len)
        cnt = (hi - lo).astype(F32)
        p = acc[HALO:HALO + tm] / cnt - e[HALO:HALO + tm]
        y = _dot(p.astype(BF16), w_ref[gi]) * sc_ref[:, gi * g:(gi + 1) * g]
        o_ref[0, :, gi * g:(gi + 1) * g] = y.astype(o_ref.dtype)


def _pool_branch(f_pool, pool_w_all, layer, pool_scale, tm):
    b, s, c = f_pool.shape
    hb = tm // HALO
    n_hb = s // HALO
    return pl.pallas_call(
        functools.partial(_pool_kernel, seq_len=s),
        grid=(b, s // tm),
        in_specs=[pl.BlockSpec((1, tm, c), lambda bi, i: (bi, i, 0)),
                  pl.BlockSpec((1, HALO, c), lambda bi, i: (bi, jnp.maximum(i * hb - 1, 0), 0)),
                  pl.BlockSpec((1, HALO, c), lambda bi, i: (bi, jnp.minimum((i + 1) * hb, n_hb - 1), 0)),
                  pl.BlockSpec((None,) + pool_w_all.shape[1:], lambda bi, i: (layer, 0, 0, 0)),
                  pl.BlockSpec((1, c), lambda bi, i: (0, 0))],
        out_specs=pl.BlockSpec((1, tm, c), lambda bi, i: (bi, i, 0)),
        out_shape=jax.ShapeDtypeStruct((b, s, c), BF16),
        compiler_params=_params("arbitrary", "arbitrary"),
        name="pool",
    )(f_pool, f_pool, f_pool, pool_w_all, pool_scale.reshape(1, c))


def _merge_kernel(h_ref, cy_ref, ay_ref, py_ref, wg0_ref, wg1_ref, wg2_ref, wc_ref, wa_ref, wp_ref,
                  o_ref):
    h = h_ref[...]
    out = _sigmoid(_dot(h, wg0_ref[...])) * _dot(cy_ref[...], wc_ref[...])
    out += _sigmoid(_dot(h, wg1_ref[...])) * _dot(ay_ref[...], wa_ref[...])
    out += _sigmoid(_dot(h, wg2_ref[...])) * _dot(py_ref[...], wp_ref[...])
    o_ref[...] = out.astype(o_ref.dtype)


def _merge(h, conv_y, attn_y, pool_y, w_gate, w_conv_out, w_mla_out, w_pool_out, layer, tm, tn):
    m, d = h.shape
    n = w_conv_out.shape[2]
    n_tiles = n // tn
    row = lambda a: pl.BlockSpec((tm, a.shape[1]), lambda i, j: (i, 0))
    col = lambda w: pl.BlockSpec((None, w.shape[1], tn), lambda i, j: (layer, 0, j))
    gate = lambda br: pl.BlockSpec((d, tn), lambda i, j: (0, br * n_tiles + j))
    return pl.pallas_call(
        _merge_kernel,
        grid=(m // tm, n_tiles),
        in_specs=[row(h), row(conv_y), row(attn_y), row(pool_y), gate(0), gate(1), gate(2),
                  col(w_conv_out), col(w_mla_out), col(w_pool_out)],
        out_specs=pl.BlockSpec((tm, tn), lambda i, j: (i, j)),
        out_shape=jax.ShapeDtypeStruct((m, n), BF16),
        compiler_params=_params("arbitrary", "arbitrary"),
        name="merge",
    )(h, conv_y, attn_y, pool_y, w_gate, w_gate, w_gate, w_conv_out, w_mla_out, w_pool_out)


def _mm_resid_kernel(a_ref, w_ref, x_ref, g_ref, o_ref):
    o_ref[0] = x_ref[0] + g_ref[0] * _dot(a_ref[0], w_ref[...])


def _matmul_resid(a, w, layer, x, g, tm, tn, name):
    b, s, k = a.shape
    n = w.shape[-1]
    if w.ndim == 3:
        wspec = pl.BlockSpec((None, k, tn), lambda bi, i, j: (layer, 0, j))
    else:
        wspec = pl.BlockSpec((k, tn), lambda bi, i, j: (0, j))
    return pl.pallas_call(
        _mm_resid_kernel,
        grid=(b, s // tm, n // tn),
        in_specs=[pl.BlockSpec((1, tm, k), lambda bi, i, j: (bi, i, 0)), wspec,
                  pl.BlockSpec((1, tm, tn), lambda bi, i, j: (bi, i, j)),
                  pl.BlockSpec((1, 1, tn), lambda bi, i, j: (bi, 0, j))],
        out_specs=pl.BlockSpec((1, tm, tn), lambda bi, i, j: (bi, i, j)),
        out_shape=jax.ShapeDtypeStruct((b, s, n), F32),
        compiler_params=_params("arbitrary", "arbitrary", "arbitrary"),
        name=name,
    )(a, w, x, g)


def _ffn_up_kernel(a_ref, wg_ref, wu_ref, *rest):
    cast_src, o_ref, cast_dst = _split_riders(rest)
    a = a_ref[...]
    gate = _dot(a, wg_ref[...])
    o_ref[...] = (gate * _sigmoid(gate) * _dot(a, wu_ref[...])).astype(o_ref.dtype)
    _run_casts(cast_src, cast_dst)


def _ffn_up(h, w_gate, w_up, tm, tn, cast=()):
    m, k = h.shape
    n = w_gate.shape[1]
    n_tiles = n // tn
    wspec = pl.BlockSpec((k, tn), lambda i, j: (0, j))
    c_in, c_out, c_shape = _cast_specs(cast, (m // tm) * n_tiles, lambda i, j: i * n_tiles + j)
    out = pl.pallas_call(
        _ffn_up_kernel,
        grid=(m // tm, n_tiles),
        in_specs=[pl.BlockSpec((tm, k), lambda i, j: (i, 0)), wspec, wspec] + c_in,
        out_specs=[pl.BlockSpec((tm, tn), lambda i, j: (i, j))] + c_out,
        out_shape=[jax.ShapeDtypeStruct((m, n), BF16)] + c_shape,
        compiler_params=_params("arbitrary", "arbitrary"),
        name="ffn_up",
    )(h, w_gate, w_up, *[src for src, _ in cast])
    return out[0], tuple(out[1:])


def _rope_tables(n_tokens, rotate):
    lane = jnp.arange(LANES)
    valid = lane < QK_ROPE_DIM
    if not rotate:
        cos = jnp.broadcast_to(valid.astype(F32), (n_tokens, LANES))
        zero = jnp.zeros((n_tokens, LANES), F32)
        return cos, zero, zero
    axis_dim = QK_ROPE_DIM // 2
    t = jnp.arange(n_tokens, dtype=jnp.int32)
    pos = jnp.stack([t // GRID_W, t % GRID_W], axis=1).astype(F32)
    inv_freq = 1.0 / (ROPE_THETA ** (jnp.arange(0, axis_dim, 2, dtype=F32) / axis_dim))
    axis = jnp.clip(lane // axis_dim, 0, 1)
    upper = (lane % axis_dim) >= axis_dim // 2
    ang = pos[:, axis] * inv_freq[lane % (axis_dim // 2)][None, :]
    cos = jnp.where(valid, jnp.cos(ang), 0.0)
    sin = jnp.sin(ang)
    sin_lo = jnp.where(valid & ~upper, -sin, 0.0)
    sin_hi = jnp.where(valid & upper, sin, 0.0)
    return cos, sin_lo, sin_hi


def _layer_weights(l, wi, w_uq, w_ukv, pool_w):
    w_kv = jnp.pad(wi[:, COL_KV:COL_POOL], ((0, 0), (0, LANES - QK_ROPE_DIM)))
    wq = w_uq[l].reshape(Q_LORA_RANK, MLA_HEADS, QK_NOPE_DIM + QK_ROPE_DIM)
    wq = jnp.pad(wq, ((0, 0), (0, 0), (0, QK_PAD - QK_NOPE_DIM - QK_ROPE_DIM))).transpose(1, 0, 2)
    return dict(
        conv=wi[:, :COL_Q].astype(BF16),
        cq=wi[:, COL_Q:COL_KV].astype(BF16),
        ckv=w_kv.astype(BF16),
        pool=wi[:, COL_POOL:COL_GATE].astype(BF16),
        gate=wi[:, COL_GATE:].astype(BF16),
        uq=wq.astype(BF16),
        ukv=w_ukv.astype(BF16),
        pool_w=pool_w.astype(BF16),
    )


def _all_layers(w):
    return w.reshape(1, w.shape[0] * w.shape[1], w.shape[2])


def _kv_path(h2d, b, s, w, layer, kv_norm_w, rope, tm, prefix=None):
    ckvn, kpe = _ckv_proj(h2d, w["ckv"], kv_norm_w, rope, tm)
    tm_kv = tm if prefix is None else prefix[0].shape[2]
    return _kv_up(ckvn.reshape(b, s, -1), kpe.reshape(b, s, -1), w["ukv"], layer, tm_kv, prefix)


def kernel(x, c, ctx, c_ctx, w_mod, b_mod, norm_mix_w, norm_ffn_w, w_in, conv_w, w_conv_out,
           q_norm_w, kv_norm_w, w_uq, w_ukv, w_mla_out, pool_w, pool_scale, w_pool_out, w_o,
           w_ffn_gate, w_ffn_up, w_ffn_down, final_norm_w):
    b, n_lat, d = x.shape
    n_ctx = ctx.shape[1]
    rope_lat = _rope_tables(n_lat, True)
    rope_ctx = _rope_tables(n_ctx, False)

    cvec = jnp.concatenate([c, c_ctx[None, :], jnp.zeros((SUBLANES - b - 1, d), F32)], axis=0)
    mod_all = _mod(cvec, w_mod, b_mod).reshape(DEPTH, SUBLANES, N_MOD, d)

    tm_lat = 512
    tm_ctx = n_ctx
    h_ctx = ctx
    out_proj = {}
    w_in_cast = None
    for l in range(DEPTH):
        last = l == DEPTH - 1
        w = _layer_weights(l, w_in[l] if w_in_cast is None else w_in_cast, w_uq, w_ukv, pool_w)
        mod_lat = [mod_all[l, :b, i].reshape(b, 1, d) for i in range(N_MOD)]
        mod_ctx = [jnp.broadcast_to(mod_all[l, b, i].reshape(1, 1, d), (b, 1, d)) for i in range(N_MOD)]

        def front(stream, mod, s, rope, tm, kv_prefix):
            h = _norm_mod(stream, norm_mix_w[l], mod[0], mod[1], tm)
            h2d = h.reshape(b * s, d)
            keys, values = _kv_path(h2d, b, s, w, l, kv_norm_w[l], rope, min(2 * tm, s), kv_prefix)
            return h2d, keys, values

        def back(stream, h2d, keys, values, mod, s, rope, tm, ffn_w):
            _, _, g1, sh2, sc2, g2 = mod
            tm_mm = 2 * tm
            tm_res = min(2 * tm, s)
            cqn = _cq_proj(h2d, w["cq"], q_norm_w[l], tm_res)
            q = _q_up(cqn.reshape(b, s, -1), w["uq"], rope, tm)
            casts, w_in_next = [], None
            if ffn_w is None:
                casts = [(w_ffn_gate, l), (w_ffn_up, l)]
                if not out_proj:
                    stacked = dict(o=w_o, mla_out=w_mla_out, conv_out=w_conv_out, pool_out=w_pool_out)
                    casts += [(_all_layers(v), 0) for v in stacked.values()]
                if not last:
                    casts.append((w_in, l + 1))
            attn, cast_out = _flash(q, keys, values, min(s, 2048), _kv_chunk(keys.shape[2]), casts)
            if ffn_w is None:
                w_gate_b, w_up_b = cast_out[:2]
                if not out_proj:
                    for (name, v), v_b in zip(stacked.items(), cast_out[2:]):
                        out_proj[name] = v_b.reshape(v.shape)
                if not last:
                    w_in_next = cast_out[-1]
            else:
                w_gate_b, w_up_b = ffn_w[:2]
            f_conv = _matmul(h2d, w["conv"], tm_mm, 1024, "in_conv")
            conv_y = _conv_branch(f_conv.reshape(b, s, -1), conv_w[l], tm)
            f_pool = _matmul(h2d, w["pool"], tm_mm, 1024, "in_pool")
            pool_y = _pool_branch(f_pool.reshape(b, s, -1), w["pool_w"], l, pool_scale[l], tm)
            merged = _merge(h2d, conv_y.reshape(b * s, -1), attn.reshape(b * s, -1),
                            pool_y.reshape(b * s, -1), w["gate"], out_proj["conv_out"],
                            out_proj["mla_out"], out_proj["pool_out"], l, tm, 512)
            x1 = _matmul_resid(merged.reshape(b, s, d), out_proj["o"], l, stream, g1, tm_res, 512, "w_o")
            h_ffn = _norm_mod(x1, norm_ffn_w[l], sh2, sc2, tm)
            hid, down_w = _ffn_up(h_ffn.reshape(b * s, d), w_gate_b, w_up_b, min(4 * tm, b * s), 256,
                                  cast=[(w_ffn_down, l)] if ffn_w is None else ())
            w_down_b = down_w[0] if ffn_w is None else ffn_w[2]
            x2 = _matmul_resid(hid.reshape(b, s, -1), w_down_b, l, x1, g2, tm, 512, "ffn_down")
            return x2, (w_gate_b, w_up_b, w_down_b), w_in_next

        hc2d, keys_c, values_c = front(h_ctx, mod_ctx, n_ctx, rope_ctx, tm_ctx, None)
        h2d, keys, values = front(x, mod_lat, n_lat, rope_lat, tm_lat, (keys_c, values_c))
        x, ffn_w, w_in_cast = back(x, h2d, keys, values, mod_lat, n_lat, rope_lat, tm_lat, None)
        if not last:
            h_ctx, _, _ = back(h_ctx, hc2d, keys_c, values_c, mod_ctx, n_ctx, rope_ctx, tm_ctx, ffn_w)

    return _final_norm(x.reshape(b * n_lat, d), final_norm_w, 512).reshape(b, n_lat, d)
```

```python
import functools

import jax
import jax.numpy as jnp
from jax import lax
from jax.experimental import pallas as pl
from jax.experimental.pallas import tpu as pltpu

F32 = jnp.float32
BF16 = jnp.bfloat16

D_MODEL = 4096
DEPTH = 2
GRID_W = 64
CONV_DIM = 1024
MLA_HEADS = 16
QK_NOPE_DIM = 128
QK_ROPE_DIM = 64
V_HEAD_DIM = 128
Q_LORA_RANK = 1024
KV_LORA_RANK = 512
ROPE_THETA = 10000.0
POOL_WINDOWS = (2, 4, 8, 16)
POOL_GROUP_DIM = 256
POOL_DIM = len(POOL_WINDOWS) * POOL_GROUP_DIM
N_BRANCHES = 3
N_MOD = 6
NORM_EPS = 1e-6

COL_Q = 3 * CONV_DIM
COL_KV = COL_Q + Q_LORA_RANK
COL_KPE = COL_KV + KV_LORA_RANK
COL_POOL = COL_KPE + QK_ROPE_DIM
COL_GATE = COL_POOL + POOL_DIM

LANES = 128
SUBLANES = 8
QK_PAD = 2 * LANES
BF16_SUBLANES = 16
HALO = BF16_SUBLANES
VMEM_LIMIT = 56 * 1024 * 1024
V_PAD = 2 * LANES
FLASH_GROUP_ROWS = 256
FLASH_MAX_CHUNK = 3072
Q_SCALE = (QK_NOPE_DIM + QK_ROPE_DIM) ** -0.5 * 1.4426950408889634


def _params(*sem):
    return pltpu.CompilerParams(dimension_semantics=sem, vmem_limit_bytes=VMEM_LIMIT)


def _sigmoid(x):
    return 1.0 / (1.0 + jnp.exp(-x))


def _dot(a, b):
    return jnp.dot(a, b, preferred_element_type=F32)


def _rms(x, w):
    ms = jnp.mean(x * x, axis=-1, keepdims=True)
    return x * lax.rsqrt(ms + NORM_EPS) * w


def _rope(pe, cos, sin_lo, sin_hi):
    return (pe * cos + pltpu.roll(pe, LANES - 16, 1) * sin_lo
            + pltpu.roll(pe, 16, 1) * sin_hi)


def _cast_specs(casts, n_steps, step):
    ins, outs, shapes = [], [], []
    for src, layer in casts:
        _, rows, cols = src.shape
        assert rows % n_steps == 0 and (rows // n_steps) % BF16_SUBLANES == 0
        r = rows // n_steps
        ins.append(pl.BlockSpec((None, r, cols), lambda *g, layer=layer: (layer, step(*g), 0)))
        outs.append(pl.BlockSpec((r, cols), lambda *g: (step(*g), 0)))
        shapes.append(jax.ShapeDtypeStruct((rows, cols), BF16))
    return ins, outs, shapes


def _split_riders(rest):
    n = (len(rest) - 1) // 2
    return rest[:n], rest[n], rest[n + 1:]


def _run_casts(srcs, dsts):
    for src, dst in zip(srcs, dsts):
        dst[...] = src[...].astype(dst.dtype)


def _mod_kernel(c_ref, w_ref, b_ref, o_ref):
    c = c_ref[...]
    a = (c * _sigmoid(c)).astype(BF16)
    o_ref[0] = _dot(a, w_ref[0].astype(BF16)) + b_ref[0]


def _mod(cvec, w_mod, b_mod):
    depth, d, n = w_mod.shape
    rows = cvec.shape[0]
    tn = 1024
    return pl.pallas_call(
        _mod_kernel,
        grid=(depth, n // tn),
        in_specs=[pl.BlockSpec((rows, d), lambda l, j: (0, 0)),
                  pl.BlockSpec((1, d, tn), lambda l, j: (l, 0, j)),
                  pl.BlockSpec((1, 1, tn), lambda l, j: (l, 0, j))],
        out_specs=pl.BlockSpec((1, rows, tn), lambda l, j: (l, 0, j)),
        out_shape=jax.ShapeDtypeStruct((depth, rows, n), F32),
        compiler_params=_params("arbitrary", "arbitrary"),
        name="mod",
    )(cvec, w_mod, b_mod.reshape(depth, 1, n))


def _norm_mod_kernel(x_ref, w_ref, sh_ref, sc_ref, o_ref):
    y = _rms(x_ref[0], w_ref[...])
    o_ref[0] = (y * (1.0 + sc_ref[0]) + sh_ref[0]).astype(o_ref.dtype)


def _norm_mod(x, w, shift, scale, tm):
    b, s, d = x.shape
    vec = pl.BlockSpec((1, 1, d), lambda bi, i: (bi, 0, 0))
    return pl.pallas_call(
        _norm_mod_kernel,
        grid=(b, s // tm),
        in_specs=[pl.BlockSpec((1, tm, d), lambda bi, i: (bi, i, 0)),
                  pl.BlockSpec((1, d), lambda bi, i: (0, 0)), vec, vec],
        out_specs=pl.BlockSpec((1, tm, d), lambda bi, i: (bi, i, 0)),
        out_shape=jax.ShapeDtypeStruct((b, s, d), BF16),
        compiler_params=_params("arbitrary", "arbitrary"),
        name="norm_mod",
    )(x, w.reshape(1, d), shift, scale)


def _norm_kernel(x_ref, w_ref, o_ref):
    o_ref[...] = _rms(x_ref[...], w_ref[...])


def _final_norm(x, w, tm):
    m, d = x.shape
    return pl.pallas_call(
        _norm_kernel,
        grid=(m // tm,),
        in_specs=[pl.BlockSpec((tm, d), lambda i: (i, 0)),
                  pl.BlockSpec((1, d), lambda i: (0, 0))],
        out_specs=pl.BlockSpec((tm, d), lambda i: (i, 0)),
        out_shape=jax.ShapeDtypeStruct((m, d), F32),
        compiler_params=_params("arbitrary"),
        name="final_norm",
    )(x, w.reshape(1, d))


def _mm_kernel(a_ref, w_ref, o_ref):
    o_ref[...] = _dot(a_ref[...], w_ref[...]).astype(o_ref.dtype)


def _matmul(a, w, tm, tn, name):
    m, k = a.shape
    n = w.shape[1]
    return pl.pallas_call(
        _mm_kernel,
        grid=(m // tm, n // tn),
        in_specs=[pl.BlockSpec((tm, k), lambda i, j: (i, 0)),
                  pl.BlockSpec((k, tn), lambda i, j: (0, j))],
        out_specs=pl.BlockSpec((tm, tn), lambda i, j: (i, j)),
        out_shape=jax.ShapeDtypeStruct((m, n), BF16),
        compiler_params=_params("arbitrary", "arbitrary"),
        name=name,
    )(a, w)


def _cq_kernel(a_ref, w_ref, nw_ref, o_ref):
    o_ref[...] = _rms(_dot(a_ref[...], w_ref[...]), nw_ref[...]).astype(o_ref.dtype)


def _cq_proj(h, w, nw, tm):
    m, k = h.shape
    n = w.shape[1]
    return pl.pallas_call(
        _cq_kernel,
        grid=(m // tm,),
        in_specs=[pl.BlockSpec((tm, k), lambda i: (i, 0)),
                  pl.BlockSpec((k, n), lambda i: (0, 0)),
                  pl.BlockSpec((1, n), lambda i: (0, 0))],
        out_specs=pl.BlockSpec((tm, n), lambda i: (i, 0)),
        out_shape=jax.ShapeDtypeStruct((m, n), BF16),
        compiler_params=_params("arbitrary"),
        name="cq_proj",
    )(h, w, nw.reshape(1, n))


def _ckv_kernel(a_ref, w_ref, nw_ref, cos_ref, slo_ref, shi_ref, ckv_ref, kpe_ref):
    r = _dot(a_ref[...], w_ref[...])
    ckv_ref[...] = _rms(r[:, :KV_LORA_RANK], nw_ref[...]).astype(ckv_ref.dtype)
    pe = _rope(r[:, KV_LORA_RANK:], cos_ref[...], slo_ref[...], shi_ref[...])
    kpe_ref[...] = pe.astype(kpe_ref.dtype)


def _ckv_proj(h, w, nw, rope, tm):
    m, k = h.shape
    n = w.shape[1]
    n_pos = rope[0].shape[0] // tm
    tab = pl.BlockSpec((tm, LANES), lambda i: (i % n_pos, 0))
    return pl.pallas_call(
        _ckv_kernel,
        grid=(m // tm,),
        in_specs=[pl.BlockSpec((tm, k), lambda i: (i, 0)),
                  pl.BlockSpec((k, n), lambda i: (0, 0)),
                  pl.BlockSpec((1, KV_LORA_RANK), lambda i: (0, 0)), tab, tab, tab],
        out_specs=[pl.BlockSpec((tm, KV_LORA_RANK), lambda i: (i, 0)),
                   pl.BlockSpec((tm, LANES), lambda i: (i, 0))],
        out_shape=[jax.ShapeDtypeStruct((m, KV_LORA_RANK), BF16),
                   jax.ShapeDtypeStruct((m, LANES), BF16)],
        compiler_params=_params("arbitrary"),
        name="ckv_proj",
    )(h, w, nw.reshape(1, KV_LORA_RANK), *rope)


def _q_up_kernel(a_ref, w_ref, cos_ref, slo_ref, shi_ref, q_ref):
    a = a_ref[0]
    cos, slo, shi = cos_ref[...], slo_ref[...], shi_ref[...]
    for h in range(MLA_HEADS):
        r = _dot(a, w_ref[h])
        q_ref[0, h, :, :LANES] = (r[:, :LANES] * Q_SCALE).astype(q_ref.dtype)
        pe = _rope(r[:, LANES:], cos, slo, shi)
        q_ref[0, h, :, LANES:] = (pe * Q_SCALE).astype(q_ref.dtype)


def _q_up(cqn, w, rope, tm):
    b, s, k = cqn.shape
    tab = pl.BlockSpec((tm, LANES), lambda bi, i: (i, 0))
    return pl.pallas_call(
        _q_up_kernel,
        grid=(b, s // tm),
        in_specs=[pl.BlockSpec((1, tm, k), lambda bi, i: (bi, i, 0)),
                  pl.BlockSpec((MLA_HEADS, k, QK_PAD), lambda bi, i: (0, 0, 0),
                               pipeline_mode=pl.Buffered(1)), tab, tab, tab],
        out_specs=pl.BlockSpec((1, MLA_HEADS, tm, QK_PAD), lambda bi, i: (bi, 0, i, 0)),
        out_shape=jax.ShapeDtypeStruct((b, MLA_HEADS, s, QK_PAD), BF16),
        compiler_params=_params("arbitrary", "arbitrary"),
        name="q_up",
    )(cqn, w, *rope)


def _kv_up_compute(a_ref, pe_ref, w_ref, k_ref, v_ref):
    a = a_ref[0]
    pe = pe_ref[0]
    width = QK_NOPE_DIM + V_HEAD_DIM
    ones = jnp.ones((a.shape[0], V_PAD - V_HEAD_DIM), v_ref.dtype)
    for h in range(MLA_HEADS):
        r = _dot(a, w_ref[:, h * width:(h + 1) * width])
        k_ref[0, h, :, :LANES] = r[:, :QK_NOPE_DIM].astype(k_ref.dtype)
        k_ref[0, h, :, LANES:] = pe
        v_ref[0, h, :, :V_HEAD_DIM] = r[:, QK_NOPE_DIM:].astype(v_ref.dtype)
        v_ref[0, h, :, V_HEAD_DIM:] = ones


def _kv_up_kernel(a_ref, pe_ref, w_ref, k_ref, v_ref):
    _kv_up_compute(a_ref, pe_ref, w_ref, k_ref, v_ref)


def _kv_up_prefix_kernel(a_ref, pe_ref, w_ref, kp_ref, vp_ref, k_ref, v_ref):
    @pl.when(pl.program_id(1) == 0)
    def _():
        k_ref[...] = kp_ref[...]
        v_ref[...] = vp_ref[...]

    @pl.when(pl.program_id(1) > 0)
    def _():
        _kv_up_compute(a_ref, pe_ref, w_ref, k_ref, v_ref)


def _kv_up(ckvn, kpe, w_all, layer, tm, prefix=None):
    b, s, k = ckvn.shape
    wspec = pl.BlockSpec((None,) + w_all.shape[1:], lambda bi, i: (layer, 0, 0))
    if prefix is None:
        n_pre, kernel, extra, extra_specs = 0, _kv_up_kernel, (), []
        row = lambda bi, i: (bi, i, 0)
    else:
        n_pre, kernel, extra = 1, _kv_up_prefix_kernel, tuple(prefix)
        assert prefix[0].shape[2] == tm
        row = lambda bi, i: (bi, jnp.maximum(i - 1, 0), 0)
        extra_specs = [pl.BlockSpec((1, MLA_HEADS, tm, QK_PAD), lambda bi, i: (bi, 0, 0, 0)),
                       pl.BlockSpec((1, MLA_HEADS, tm, V_PAD), lambda bi, i: (bi, 0, 0, 0))]
    s_out = s + n_pre * tm
    return pl.pallas_call(
        kernel,
        grid=(b, s_out // tm),
        in_specs=[pl.BlockSpec((1, tm, k), row), pl.BlockSpec((1, tm, LANES), row), wspec] + extra_specs,
        out_specs=[pl.BlockSpec((1, MLA_HEADS, tm, QK_PAD), lambda bi, i: (bi, 0, i, 0)),
                   pl.BlockSpec((1, MLA_HEADS, tm, V_PAD), lambda bi, i: (bi, 0, i, 0))],
        out_shape=[jax.ShapeDtypeStruct((b, MLA_HEADS, s_out, QK_PAD), BF16),
                   jax.ShapeDtypeStruct((b, MLA_HEADS, s_out, V_PAD), BF16)],
        compiler_params=_params("arbitrary", "arbitrary"),
        name="kv_up",
    )(ckvn, kpe, w_all, *extra)


def _flash_init(m_ref, acc_ref):
    m_ref[...] = jnp.full(m_ref.shape, -jnp.inf, F32)
    acc_ref[...] = jnp.zeros(acc_ref.shape, F32)


def _flash_update(q_ref, k, v, m_ref, acc_ref):
    tq = q_ref.shape[2]
    n_blk = k.shape[0] // LANES
    for g in range(tq // FLASH_GROUP_ROWS):
        rows = slice(g * FLASH_GROUP_ROWS, (g + 1) * FLASH_GROUP_ROWS)
        s = lax.dot_general(q_ref[0, 0, rows, :], k, (((1,), (1,)), ((), ())),
                            preferred_element_type=F32)
        blocks = [s[:, c * LANES:(c + 1) * LANES] for c in range(n_blk)]
        m_blk = functools.reduce(jnp.maximum, blocks)
        m_prev = m_ref[rows, :]
        m_new = jnp.maximum(m_prev, jnp.max(m_blk, axis=1, keepdims=True))
        p = jnp.concatenate([jnp.exp2(blk - m_new).astype(BF16) for blk in blocks], axis=1)
        alpha = jnp.exp2(m_prev - m_new)
        acc_ref[rows, :] = acc_ref[rows, :] * jnp.concatenate([alpha, alpha], axis=1) + _dot(p, v)
        m_ref[rows, :] = m_new


def _flash_finish(o_ref, acc_ref):
    acc = acc_ref[...]
    o_ref[0] = (acc[:, :V_HEAD_DIM] / acc[:, V_HEAD_DIM:]).astype(o_ref.dtype)


def _flash_kernel(q_ref, k_ref, v_ref, *rest, tk):
    cast_src, o_ref, cast_dst = _split_riders(rest[:-2])
    m_ref, acc_ref = rest[-2:]
    _flash_init(m_ref, acc_ref)
    for j in range(k_ref.shape[2] // tk):
        rows = slice(j * tk, (j + 1) * tk)
        _flash_update(q_ref, k_ref[0, 0, rows, :], v_ref[0, 0, rows, :], m_ref, acc_ref)
    _flash_finish(o_ref, acc_ref)
    _run_casts(cast_src, cast_dst)


def _kv_chunk(n_kv):
    mxu = 2 * LANES
    return max(t for t in range(mxu, FLASH_MAX_CHUNK + 1, mxu) if n_kv % t == 0)


def _flash(q, k, v, tq, tk, cast=()):
    b, nh, s, _ = q.shape
    n_kv = k.shape[2]
    n_q = s // tq
    whole = lambda w: pl.BlockSpec((1, 1, n_kv, w), lambda bi, h, i: (bi, h, 0, 0))
    c_in, c_out, c_shape = _cast_specs(cast, b * nh * n_q, lambda bi, h, i: (bi * nh + h) * n_q + i)
    out = pl.pallas_call(
        functools.partial(_flash_kernel, tk=tk),
        grid=(b, nh, n_q),
        in_specs=[pl.BlockSpec((1, 1, tq, QK_PAD), lambda bi, h, i: (bi, h, i, 0)),
                  whole(QK_PAD), whole(V_PAD)] + c_in,
        out_specs=[pl.BlockSpec((1, tq, V_HEAD_DIM), lambda bi, h, i: (bi, i, h))] + c_out,
        out_shape=[jax.ShapeDtypeStruct((b, s, nh * V_HEAD_DIM), BF16)] + c_shape,
        scratch_shapes=[pltpu.VMEM((tq, LANES), F32), pltpu.VMEM((tq, V_PAD), F32)],
        compiler_params=_params("arbitrary", "arbitrary", "arbitrary"),
        name="flash",
    )(q, k, v, *[src for src, _ in cast])
    return out[0], tuple(out[1:])


def _conv_kernel(bg_ref, cg_ref, xin_ref, cgp_ref, xinp_ref, cgn_ref, xinn_ref, w_ref, o_ref):
    i = pl.program_id(1)
    tm = cg_ref.shape[1]
    u = cg_ref[0].astype(F32) * xin_ref[0].astype(F32)
    u_prev = cgp_ref[0, HALO - 1:HALO, :].astype(F32) * xinp_ref[0, HALO - 1:HALO, :].astype(F32)
    u_next = cgn_ref[0, 0:1, :].astype(F32) * xinn_ref[0, 0:1, :].astype(F32)
    u_prev = jnp.where(i > 0, u_prev, 0.0)
    u_next = jnp.where(i < pl.num_programs(1) - 1, u_next, 0.0)
    row = lax.broadcasted_iota(jnp.int32, u.shape, 0)
    below = jnp.where(row == 0, u_prev, pltpu.roll(u, 1, 0))
    above = jnp.where(row == tm - 1, u_next, pltpu.roll(u, tm - 1, 0))
    y = below * w_ref[0:1, :] + u * w_ref[1:2, :] + above * w_ref[2:3, :]
    o_ref[0] = (bg_ref[0].astype(F32) * y).astype(o_ref.dtype)


def _conv_branch(f_conv, conv_w, tm):
    b, s, _ = f_conv.shape
    c = CONV_DIM
    hb = tm // HALO
    n_hb = s // HALO
    main = lambda col: pl.BlockSpec((1, tm, c), lambda bi, i: (bi, i, col))
    prev = lambda col: pl.BlockSpec((1, HALO, c), lambda bi, i: (bi, jnp.maximum(i * hb - 1, 0), col))
    nxt = lambda col: pl.BlockSpec((1, HALO, c), lambda bi, i: (bi, jnp.minimum((i + 1) * hb, n_hb - 1), col))
    return pl.pallas_call(
        _conv_kernel,
        grid=(b, s // tm),
        in_specs=[main(0), main(1), main(2), prev(1), prev(2), nxt(1), nxt(2),
                  pl.BlockSpec(conv_w.shape, lambda bi, i: (0, 0))],
        out_specs=pl.BlockSpec((1, tm, c), lambda bi, i: (bi, i, 0)),
        out_shape=jax.ShapeDtypeStruct((b, s, c), BF16),
        compiler_params=_params("arbitrary", "arbitrary"),
        name="conv",
    )(f_conv, f_conv, f_conv, f_conv, f_conv, f_conv, f_conv, conv_w)


def _pool_kernel(u_ref, up_ref, un_ref, w_ref, sc_ref, o_ref, *, seq_len):
    i = pl.program_id(1)
    tm = u_ref.shape[1]
    rows = tm + 2 * HALO
    prev = jnp.where(i > 0, up_ref[0].astype(F32), 0.0)
    nxt = jnp.where(i < pl.num_programs(1) - 1, un_ref[0].astype(F32), 0.0)
    ext = jnp.concatenate([prev, u_ref[0].astype(F32), nxt], axis=0)
    t = i * tm + lax.broadcasted_iota(jnp.int32, (tm, 1), 0)
    g = POOL_GROUP_DIM
    for gi, win in enumerate(POOL_WINDOWS):
        e = ext[:, gi * g:(gi + 1) * g]
        acc = e + pltpu.roll(e, 1, 0)
        half = 1
        while 2 * half < win:
            acc = pltpu.roll(acc, half, 0) + pltpu.roll(acc, rows - half, 0)
            half *= 2
        lo = jnp.maximum(t - win // 2, 0)
        hi = jnp.minimum(t + (win - win // 2), seq_len)
        cnt = (hi - lo).astype(F32)
        p = acc[HALO:HALO + tm] / cnt - e[HALO:HALO + tm]
        y = _dot(p.astype(BF16), w_ref[gi]) * sc_ref[:, gi * g:(gi + 1) * g]
        o_ref[0, :, gi * g:(gi + 1) * g] = y.astype(o_ref.dtype)


def _pool_branch(f_pool, pool_w_all, layer, pool_scale, tm):
    b, s, c = f_pool.shape
    hb = tm // HALO
    n_hb = s // HALO
    return pl.pallas_call(
        functools.partial(_pool_kernel, seq_len=s),
        grid=(b, s // tm),
        in_specs=[pl.BlockSpec((1, tm, c), lambda bi, i: (bi, i, 0)),
                  pl.BlockSpec((1, HALO, c), lambda bi, i: (bi, jnp.maximum(i * hb - 1, 0), 0)),
                  pl.BlockSpec((1, HALO, c), lambda bi, i: (bi, jnp.minimum((i + 1) * hb, n_hb - 1), 0)),
                  pl.BlockSpec((None,) + pool_w_all.shape[1:], lambda bi, i: (layer, 0, 0, 0)),
                  pl.BlockSpec((1, c), lambda bi, i: (0, 0))],
        out_specs=pl.BlockSpec((1, tm, c), lambda bi, i: (bi, i, 0)),
        out_shape=jax.ShapeDtypeStruct((b, s, c), BF16),
        compiler_params=_params("arbitrary", "arbitrary"),
        name="pool",
    )(f_pool, f_pool, f_pool, pool_w_all, pool_scale.reshape(1, c))


def _merge_kernel(h_ref, cy_ref, ay_ref, py_ref, wg0_ref, wg1_ref, wg2_ref, wc_ref, wa_ref, wp_ref,
                  o_ref):
    h = h_ref[...]
    out = _sigmoid(_dot(h, wg0_ref[...])) * _dot(cy_ref[...], wc_ref[...])
    out += _sigmoid(_dot(h, wg1_ref[...])) * _dot(ay_ref[...], wa_ref[...])
    out += _sigmoid(_dot(h, wg2_ref[...])) * _dot(py_ref[...], wp_ref[...])
    o_ref[...] = out.astype(o_ref.dtype)


def _merge(h, conv_y, attn_y, pool_y, w_gate, w_conv_out, w_mla_out, w_pool_out, layer, tm, tn):
    m, d = h.shape
    n = w_conv_out.shape[2]
    n_tiles = n // tn
    row = lambda a: pl.BlockSpec((tm, a.shape[1]), lambda i, j: (i, 0))
    col = lambda w: pl.BlockSpec((None, w.shape[1], tn), lambda i, j: (layer, 0, j))
    gate = lambda br: pl.BlockSpec((d, tn), lambda i, j: (0, br * n_tiles + j))
    return pl.pallas_call(
        _merge_kernel,
        grid=(m // tm, n_tiles),
        in_specs=[row(h), row(conv_y), row(attn_y), row(pool_y), gate(0), gate(1), gate(2),
                  col(w_conv_out), col(w_mla_out), col(w_pool_out)],
        out_specs=pl.BlockSpec((tm, tn), lambda i, j: (i, j)),
        out_shape=jax.ShapeDtypeStruct((m, n), BF16),
        compiler_params=_params("arbitrary", "arbitrary"),
        name="merge",
    )(h, conv_y, attn_y, pool_y, w_gate, w_gate, w_gate, w_conv_out, w_mla_out, w_pool_out)


def _mm_resid_kernel(a_ref, w_ref, x_ref, g_ref, o_ref):
    o_ref[0] = x_ref[0] + g_ref[0] * _dot(a_ref[0], w_ref[...])


def _matmul_resid(a, w, layer, x, g, tm, tn, name):
    b, s, k = a.shape
    n = w.shape[-1]
    if w.ndim == 3:
        wspec = pl.BlockSpec((None, k, tn), lambda bi, i, j: (layer, 0, j))
    else:
        wspec = pl.BlockSpec((k, tn), lambda bi, i, j: (0, j))
    return pl.pallas_call(
        _mm_resid_kernel,
        grid=(b, s // tm, n // tn),
        in_specs=[pl.BlockSpec((1, tm, k), lambda bi, i, j: (bi, i, 0)), wspec,
                  pl.BlockSpec((1, tm, tn), lambda bi, i, j: (bi, i, j)),
                  pl.BlockSpec((1, 1, tn), lambda bi, i, j: (bi, 0, j))],
        out_specs=pl.BlockSpec((1, tm, tn), lambda bi, i, j: (bi, i, j)),
        out_shape=jax.ShapeDtypeStruct((b, s, n), F32),
        compiler_params=_params("arbitrary", "arbitrary", "arbitrary"),
        name=name,
    )(a, w, x, g)


def _ffn_up_kernel(a_ref, wg_ref, wu_ref, *rest):
    cast_src, o_ref, cast_dst = _split_riders(rest)
    a = a_ref[...]
    gate = _dot(a, wg_ref[...])
    o_ref[...] = (gate * _sigmoid(gate) * _dot(a, wu_ref[...])).astype(o_ref.dtype)
    _run_casts(cast_src, cast_dst)


def _ffn_up(h, w_gate, w_up, tm, tn, cast=()):
    m, k = h.shape
    n = w_gate.shape[1]
    n_tiles = n // tn
    wspec = pl.BlockSpec((k, tn), lambda i, j: (0, j))
    c_in, c_out, c_shape = _cast_specs(cast, (m // tm) * n_tiles, lambda i, j: i * n_tiles + j)
    out = pl.pallas_call(
        _ffn_up_kernel,
        grid=(m // tm, n_tiles),
        in_specs=[pl.BlockSpec((tm, k), lambda i, j: (i, 0)), wspec, wspec] + c_in,
        out_specs=[pl.BlockSpec((tm, tn), lambda i, j: (i, j))] + c_out,
        out_shape=[jax.ShapeDtypeStruct((m, n), BF16)] + c_shape,
        compiler_params=_params("arbitrary", "arbitrary"),
        name="ffn_up",
    )(h, w_gate, w_up, *[src for src, _ in cast])
    return out[0], tuple(out[1:])


def _rope_tables(n_tokens, rotate):
    lane = jnp.arange(LANES)
    valid = lane < QK_ROPE_DIM
    if not rotate:
        cos = jnp.broadcast_to(valid.astype(F32), (n_tokens, LANES))
        zero = jnp.zeros((n_tokens, LANES), F32)
        return cos, zero, zero
    axis_dim = QK_ROPE_DIM // 2
    t = jnp.arange(n_tokens, dtype=jnp.int32)
    pos = jnp.stack([t // GRID_W, t % GRID_W], axis=1).astype(F32)
    inv_freq = 1.0 / (ROPE_THETA ** (jnp.arange(0, axis_dim, 2, dtype=F32) / axis_dim))
    axis = jnp.clip(lane // axis_dim, 0, 1)
    upper = (lane % axis_dim) >= axis_dim // 2
    ang = pos[:, axis] * inv_freq[lane % (axis_dim // 2)][None, :]
    cos = jnp.where(valid, jnp.cos(ang), 0.0)
    sin = jnp.sin(ang)
    sin_lo = jnp.where(valid & ~upper, -sin, 0.0)
    sin_hi = jnp.where(valid & upper, sin, 0.0)
    return cos, sin_lo, sin_hi


def _layer_weights(l, wi, w_uq, w_ukv, pool_w):
    w_kv = jnp.pad(wi[:, COL_KV:COL_POOL], ((0, 0), (0, LANES - QK_ROPE_DIM)))
    wq = w_uq[l].reshape(Q_LORA_RANK, MLA_HEADS, QK_NOPE_DIM + QK_ROPE_DIM)
    wq = jnp.pad(wq, ((0, 0), (0, 0), (0, QK_PAD - QK_NOPE_DIM - QK_ROPE_DIM))).transpose(1, 0, 2)
    return dict(
        conv=wi[:, :COL_Q].astype(BF16),
        cq=wi[:, COL_Q:COL_KV].astype(BF16),
        ckv=w_kv.astype(BF16),
        pool=wi[:, COL_POOL:COL_GATE].astype(BF16),
        gate=wi[:, COL_GATE:].astype(BF16),
        uq=wq.astype(BF16),
        ukv=w_ukv.astype(BF16),
        pool_w=pool_w.astype(BF16),
    )


def _all_layers(w):
    return w.reshape(1, w.shape[0] * w.shape[1], w.shape[2])


def _kv_path(h2d, b, s, w, layer, kv_norm_w, rope, tm, prefix=None):
    ckvn, kpe = _ckv_proj(h2d, w["ckv"], kv_norm_w, rope, tm)
    tm_kv = tm if prefix is None else prefix[0].shape[2]
    return _kv_up(ckvn.reshape(b, s, -1), kpe.reshape(b, s, -1), w["ukv"], layer, tm_kv, prefix)


def kernel(x, c, ctx, c_ctx, w_mod, b_mod, norm_mix_w, norm_ffn_w, w_in, conv_w, w_conv_out,
           q_norm_w, kv_norm_w, w_uq, w_ukv, w_mla_out, pool_w, pool_scale, w_pool_out, w_o,
           w_ffn_gate, w_ffn_up, w_ffn_down, final_norm_w):
    b, n_lat, d = x.shape
    n_ctx = ctx.shape[1]
    rope_lat = _rope_tables(n_lat, True)
    rope_ctx = _rope_tables(n_ctx, False)

    cvec = jnp.concatenate([c, c_ctx[None, :], jnp.zeros((SUBLANES - b - 1, d), F32)], axis=0)
    mod_all = _mod(cvec, w_mod, b_mod).reshape(DEPTH, SUBLANES, N_MOD, d)

    tm_lat = 512
    tm_ctx = n_ctx
    h_ctx = ctx
    out_proj = {}
    for l in range(DEPTH):
        last = l == DEPTH - 1
        w = _layer_weights(l, w_in[l], w_uq, w_ukv, pool_w)
        mod_lat = [mod_all[l, :b, i].reshape(b, 1, d) for i in range(N_MOD)]
        mod_ctx = [jnp.broadcast_to(mod_all[l, b, i].reshape(1, 1, d), (b, 1, d)) for i in range(N_MOD)]

        def front(stream, mod, s, rope, tm, kv_prefix):
            h = _norm_mod(stream, norm_mix_w[l], mod[0], mod[1], tm)
            h2d = h.reshape(b * s, d)
            keys, values = _kv_path(h2d, b, s, w, l, kv_norm_w[l], rope, min(2 * tm, s), kv_prefix)
            return h2d, keys, values

        def back(stream, h2d, keys, values, mod, s, rope, tm, ffn_w):
            _, _, g1, sh2, sc2, g2 = mod
            tm_mm = 2 * tm
            tm_res = min(2 * tm, s)
            cqn = _cq_proj(h2d, w["cq"], q_norm_w[l], tm_res)
            q = _q_up(cqn.reshape(b, s, -1), w["uq"], rope, tm_res)
            casts = []
            if ffn_w is None:
                casts = [(w_ffn_gate, l), (w_ffn_up, l)]
                if not out_proj:
                    stacked = dict(o=w_o, mla_out=w_mla_out, conv_out=w_conv_out, pool_out=w_pool_out)
                    casts += [(_all_layers(v), 0) for v in stacked.values()]
            attn, cast_out = _flash(q, keys, values, min(s, 2048), _kv_chunk(keys.shape[2]), casts)
            if ffn_w is None:
                w_gate_b, w_up_b = cast_out[:2]
                if not out_proj:
                    for (name, v), v_b in zip(stacked.items(), cast_out[2:]):
                        out_proj[name] = v_b.reshape(v.shape)
            else:
                w_gate_b, w_up_b = ffn_w[:2]
            f_conv = _matmul(h2d, w["conv"], tm_mm, 1024, "in_conv")
            conv_y = _conv_branch(f_conv.reshape(b, s, -1), conv_w[l], tm)
            f_pool = _matmul(h2d, w["pool"], tm_mm, 1024, "in_pool")
            pool_y = _pool_branch(f_pool.reshape(b, s, -1), w["pool_w"], l, pool_scale[l], tm)
            merged = _merge(h2d, conv_y.reshape(b * s, -1), attn.reshape(b * s, -1),
                            pool_y.reshape(b * s, -1), w["gate"], out_proj["conv_out"],
                            out_proj["mla_out"], out_proj["pool_out"], l, tm, 512)
            x1 = _matmul_resid(merged.reshape(b, s, d), out_proj["o"], l, stream, g1, tm_res, 512, "w_o")
            h_ffn = _norm_mod(x1, norm_ffn_w[l], sh2, sc2, tm)
            hid, down_w = _ffn_up(h_ffn.reshape(b * s, d), w_gate_b, w_up_b, min(4 * tm, b * s), 256,
                                  cast=[(w_ffn_down, l)] if ffn_w is None else ())
            w_down_b = down_w[0] if ffn_w is None else ffn_w[2]
            x2 = _matmul_resid(hid.reshape(b, s, -1), w_down_b, l, x1, g2, tm, 512, "ffn_down")
            return x2, (w_gate_b, w_up_b, w_down_b)

        hc2d, keys_c, values_c = front(h_ctx, mod_ctx, n_ctx, rope_ctx, tm_ctx, None)
        h2d, keys, values = front(x, mod_lat, n_lat, rope_lat, tm_lat, (keys_c, values_c))
        x, ffn_w = back(x, h2d, keys, values, mod_lat, n_lat, rope_lat, tm_lat, None)
        if not last:
            h_ctx, _ = back(h_ctx, hc2d, keys_c, values_c, mod_ctx, n_ctx, rope_ctx, tm_ctx, ffn_w)

    return _final_norm(x.reshape(b * n_lat, d), final_norm_w, 512).reshape(b, n_lat, d)
```
